```python
import math
import jax, jax.numpy as jnp
from jax import lax
import numpy as np

D_MODEL = 4096
BATCH = 1
SEQ = 16384
DEPTH = 2

F_GROUPS = 4
F_GROUP_DIM = 256
F_WIDTH = F_GROUPS * F_GROUP_DIM
MLA_HEADS = 8
QK_NOPE = 128
QK_ROPE = 64
V_DIM = 128
Q_LORA = 768
KV_LORA = 512
ROPE_THETA = 10000.0
Q_BLOCK = 128
MLA_WIDTH = MLA_HEADS * V_DIM
SG_GROUPS = 16
SG_GROUP_DIM = 128
SG_WIDTH = SG_GROUPS * SG_GROUP_DIM
CHUNK = 128
N_BRANCH = 3
OFF_F = 0
OFF_CQ = OFF_F + F_WIDTH
OFF_CKV = OFF_CQ + Q_LORA
OFF_KR = OFF_CKV + KV_LORA
OFF_SG = OFF_KR + QK_ROPE
N_IN = OFF_SG + 2 * SG_WIDTH
MEM_TOKENS = 256
MEM_HEADS = 4
MEM_HEAD_DIM = 256
MEM_WIDTH = MEM_HEADS * MEM_HEAD_DIM
D_FF = 11008
CONV_WIDTH = 3
EPS = 1e-6

kernel_name = "hybrid_fourier_mla_sgmlp_encoder"


def rms_norm(x, g):
    xf = x.astype(jnp.float32)
    y = xf * lax.rsqrt(jnp.mean(xf * xf, axis=-1, keepdims=True) + EPS)
    return (y * g.astype(jnp.float32)).astype(x.dtype)


def layer_norm(x, g):
    xf = x.astype(jnp.float32)
    mu = jnp.mean(xf, axis=-1, keepdims=True)
    xc = xf - mu
    y = xc * lax.rsqrt(jnp.mean(xc * xc, axis=-1, keepdims=True) + EPS)
    return (y * g.astype(jnp.float32)).astype(x.dtype)


def apply_rope(t, cos, sin):
    cos = cos.astype(t.dtype)
    sin = sin.astype(t.dtype)
    t1, t2 = t[..., : QK_ROPE // 2], t[..., QK_ROPE // 2:]
    return jnp.concatenate([t1 * cos - t2 * sin, t2 * cos + t1 * sin], axis=-1)


def fourier_branch(z):
    B, S, _ = z.shape
    zg = z.reshape(B, S, F_GROUPS, F_GROUP_DIM).astype(jnp.float32)
    y = jnp.fft.fft2(zg, axes=(1, 3), norm="ortho").real
    return y.reshape(B, S, F_WIDTH).astype(z.dtype)


def mla_branch(c_q, c_kv, k_rope, cos, sin, q_norm, w_uq, kv_norm, w_ukv):
    B, S, _ = c_q.shape
    q = (rms_norm(c_q, q_norm) @ w_uq).reshape(B, S, MLA_HEADS, QK_NOPE + QK_ROPE)
    q_nope = q[..., :QK_NOPE]
    q_rope = apply_rope(q[..., QK_NOPE:], cos[:, :, None, :], sin[:, :, None, :])
    kv = (rms_norm(c_kv, kv_norm) @ w_ukv).reshape(B, S, MLA_HEADS, QK_NOPE + V_DIM)
    k_nope, v = kv[..., :QK_NOPE], kv[..., QK_NOPE:]
    k_rope = apply_rope(k_rope, cos, sin)
    scale = 1.0 / math.sqrt(QK_NOPE + QK_ROPE)
    nb = S // Q_BLOCK

    def to_blocks(t):
        return t.reshape(B, nb, Q_BLOCK, MLA_HEADS, t.shape[-1]).swapaxes(0, 1)

    def attend(qs):
        qn, qr = qs
        s = (jnp.einsum('bqhd,bkhd->bhqk', qn, k_nope)
             + jnp.einsum('bqhr,bkr->bhqk', qr, k_rope))
        p = jax.nn.softmax(s.astype(jnp.float32) * scale, axis=-1).astype(v.dtype)
        return jnp.einsum('bhqk,bkhd->bqhd', p, v)

    o = lax.map(attend, (to_blocks(q_nope), to_blocks(q_rope)))
    return o.swapaxes(0, 1).reshape(B, S, MLA_WIDTH)


def spatial_gating_branch(z, ln_gain, w_spatial, b_spatial):
    B, S, _ = z.shape
    z = jax.nn.gelu(z)
    u, v = z[..., :SG_WIDTH], z[..., SG_WIDTH:]
    v = layer_norm(v.reshape(B, S, SG_GROUPS, SG_GROUP_DIM),
                   ln_gain.reshape(SG_GROUPS, SG_GROUP_DIM))
    v = v.reshape(B, S // CHUNK, CHUNK, SG_GROUPS, SG_GROUP_DIM)
    s = (jnp.einsum('gpq,bcqgd->bcpgd', w_spatial, v)
         + b_spatial.T[None, None, :, :, None])
    return u * s.reshape(B, S, SG_WIDTH)


def memory_cross_attention(h, mem_n, w_mq, w_mk, w_mv, w_mo):
    B, S, _ = h.shape
    q = (h @ w_mq).reshape(B, S, MEM_HEADS, MEM_HEAD_DIM)
    k = (mem_n @ w_mk).reshape(B, MEM_TOKENS, MEM_HEADS, MEM_HEAD_DIM)
    v = (mem_n @ w_mv).reshape(B, MEM_TOKENS, MEM_HEADS, MEM_HEAD_DIM)
    s = jnp.einsum('bqhd,bmhd->bhqm', q, k).astype(jnp.float32) / math.sqrt(MEM_HEAD_DIM)
    p = jax.nn.softmax(s, axis=-1).astype(v.dtype)
    o = jnp.einsum('bhqm,bmhd->bqhd', p, v).reshape(B, S, MEM_WIDTH)
    return o @ w_mo


def conv_ffn(h, w_up, conv_w, conv_b, w_down):
    u = h @ w_up
    up = jnp.pad(u, ((0, 0), (1, 1), (0, 0)))
    S = h.shape[1]
    uc = (up[:, 0:S] * conv_w[0] + up[:, 1:S + 1] * conv_w[1]
          + up[:, 2:S + 2] * conv_w[2] + conv_b)
    g, val = uc[..., :D_FF], uc[..., D_FF:]
    return (jax.nn.gelu(g) * val) @ w_down


def setup_inputs(seed: int = 0) -> dict:
    key = jax.random.key(seed)
    ks = iter(jax.random.split(key, 64))
    L = DEPTH

    def nrm(shape, scale):
        return jax.random.normal(next(ks), shape, jnp.float32) * scale

    def gain(shape):
        return 1.0 + nrm(shape, 0.02)

    x = nrm((BATCH, SEQ, D_MODEL), 1.0)
    mem = nrm((BATCH, MEM_TOKENS, D_MODEL), 1.0)
    offs = jax.random.randint(next(ks), (BATCH, 1), 0, 1024, dtype=jnp.int32)
    positions = jnp.arange(SEQ, dtype=jnp.int32)[None, :] + offs
    return {
        "x": x,
        "mem": mem,
        "positions": positions,
        "mix_pre_norm": gain((L, D_MODEL)),
        "mix_post_norm": gain((L, D_MODEL)),
        "w_in": nrm((L, D_MODEL, N_IN), D_MODEL ** -0.5),
        "mla_q_norm": gain((L, Q_LORA)),
        "w_uq": nrm((L, Q_LORA, MLA_HEADS * (QK_NOPE + QK_ROPE)), Q_LORA ** -0.5),
        "mla_kv_norm": gain((L, KV_LORA)),
        "w_ukv": nrm((L, KV_LORA, MLA_HEADS * (QK_NOPE + V_DIM)), KV_LORA ** -0.5),
        "sg_norm": gain((L, SG_WIDTH)),
        "w_spatial": nrm((L, SG_GROUPS, CHUNK, CHUNK), CHUNK ** -0.5),
        "b_spatial": 1.0 + nrm((L, SG_GROUPS, CHUNK), 0.02),
        "w_br_f": nrm((L, F_WIDTH, D_MODEL), F_WIDTH ** -0.5),
        "w_br_a": nrm((L, MLA_WIDTH, D_MODEL), MLA_WIDTH ** -0.5),
        "w_br_s": nrm((L, SG_WIDTH, D_MODEL), SG_WIDTH ** -0.5),
        "w_gate": nrm((L, D_MODEL, N_BRANCH * D_MODEL), D_MODEL ** -0.5),
        "b_gate": nrm((L, N_BRANCH * D_MODEL), 0.02),
        "w_out": nrm((L, D_MODEL, D_MODEL), D_MODEL ** -0.5),
        "mem_pre_norm": gain((L, D_MODEL)),
        "mem_post_norm": gain((L, D_MODEL)),
        "mem_kv_norm": gain((L, D_MODEL)),
        "w_mq": nrm((L, D_MODEL, MEM_WIDTH), D_MODEL ** -0.5),
        "w_mk": nrm((L, D_MODEL, MEM_WIDTH), D_MODEL ** -0.5),
        "w_mv": nrm((L, D_MODEL, MEM_WIDTH), D_MODEL ** -0.5),
        "w_mo": nrm((L, MEM_WIDTH, D_MODEL), MEM_WIDTH ** -0.5),
        "ffn_pre_norm": gain((L, D_MODEL)),
        "ffn_post_norm": gain((L, D_MODEL)),
        "w_up": nrm((L, D_MODEL, 2 * D_FF), D_MODEL ** -0.5),
        "conv_w": nrm((L, CONV_WIDTH, 2 * D_FF), CONV_WIDTH ** -0.5),
        "conv_b": nrm((L, 2 * D_FF), 0.02),
        "w_down": nrm((L, D_FF, D_MODEL), D_FF ** -0.5),
    }


def reference(x, mem, positions, mix_pre_norm, mix_post_norm, w_in, mla_q_norm, w_uq,
              mla_kv_norm, w_ukv, sg_norm, w_spatial, b_spatial, w_br_f, w_br_a, w_br_s,
              w_gate, b_gate, w_out, mem_pre_norm, mem_post_norm, mem_kv_norm, w_mq, w_mk,
              w_mv, w_mo, ffn_pre_norm, ffn_post_norm, w_up, conv_w, conv_b, w_down):
    B, S, _ = x.shape
    inv_freq = ROPE_THETA ** (-jnp.arange(0, QK_ROPE, 2, dtype=jnp.float32) / QK_ROPE)
    ang = positions.astype(jnp.float32)[..., None] * inv_freq
    cos, sin = jnp.cos(ang), jnp.sin(ang)

    for l in range(DEPTH):
        h = rms_norm(x, mix_pre_norm[l])
        z = h @ w_in[l]
        y_f = fourier_branch(z[..., OFF_F:OFF_CQ]) @ w_br_f[l]
        y_a = mla_branch(z[..., OFF_CQ:OFF_CKV], z[..., OFF_CKV:OFF_KR],
                         z[..., OFF_KR:OFF_SG], cos, sin,
                         mla_q_norm[l], w_uq[l], mla_kv_norm[l], w_ukv[l]) @ w_br_a[l]
        y_s = spatial_gating_branch(z[..., OFF_SG:], sg_norm[l], w_spatial[l],
                                    b_spatial[l]) @ w_br_s[l]
        gates = jax.nn.sigmoid((h @ w_gate[l] + b_gate[l]).astype(jnp.float32))
        gates = gates.astype(x.dtype).reshape(B, S, N_BRANCH, D_MODEL)
        merged = gates[:, :, 0] * y_f + gates[:, :, 1] * y_a + gates[:, :, 2] * y_s
        x = x + rms_norm(merged @ w_out[l], mix_post_norm[l])

        h = rms_norm(x, mem_pre_norm[l])
        mem_n = rms_norm(mem, mem_kv_norm[l])
        y = memory_cross_attention(h, mem_n, w_mq[l], w_mk[l], w_mv[l], w_mo[l])
        x = x + rms_norm(y, mem_post_norm[l])

        h = rms_norm(x, ffn_pre_norm[l])
        y = conv_ffn(h, w_up[l], conv_w[l], conv_b[l], w_down[l])
        x = x + rms_norm(y, ffn_post_norm[l])
    return x
```

```python
import functools
import math

import jax
import jax.numpy as jnp
from jax import lax
from jax.experimental import pallas as pl
from jax.experimental.pallas import tpu as pltpu

F32 = jnp.float32
BF16 = jnp.bfloat16
EPS = 1e-6

V7X_VMEM_BYTES = 64 * 1024 * 1024
VMEM_LIMIT_BYTES = V7X_VMEM_BYTES - 8 * 1024 * 1024
LANES = 128

F_GROUPS = 4
F_GROUP_DIM = 256
F_WIDTH = F_GROUPS * F_GROUP_DIM
MLA_HEADS = 8
QK_NOPE = 128
QK_ROPE = 64
V_DIM = 128
Q_LORA = 768
KV_LORA = 512
ROPE_THETA = 10000.0
SG_GROUPS = 16
SG_GROUP_DIM = 128
SG_WIDTH = SG_GROUPS * SG_GROUP_DIM
CHUNK = 128
MEM_HEADS = 4
MEM_HEAD_DIM = 256
OFF_CQ = F_WIDTH
OFF_CKV = OFF_CQ + Q_LORA
OFF_KR = OFF_CKV + KV_LORA
OFF_SG = OFF_KR + QK_ROPE


def _params(*sem):
    return pltpu.CompilerParams(dimension_semantics=sem, vmem_limit_bytes=VMEM_LIMIT_BYTES)


def _tile(n, pref):
    if n <= pref:
        return n
    t = pref
    while n % t:
        t //= 2
    return t


def _row(v):
    return v.reshape(1, -1).astype(F32)


def _rmsnorm_kernel(x_ref, g_ref, o_ref):
    x = x_ref[...].astype(F32)
    ms = jnp.mean(x * x, axis=-1, keepdims=True)
    o_ref[...] = (x * lax.rsqrt(ms + EPS) * g_ref[...]).astype(o_ref.dtype)


def rmsnorm(x, g, out_dtype=BF16):
    s, d = x.shape
    tm = _tile(s, 512)
    return pl.pallas_call(
        _rmsnorm_kernel,
        grid=(s // tm,),
        in_specs=[pl.BlockSpec((tm, d), lambda i: (i, 0)), pl.BlockSpec((1, d), lambda i: (0, 0))],
        out_specs=pl.BlockSpec((tm, d), lambda i: (i, 0)),
        out_shape=jax.ShapeDtypeStruct((s, d), out_dtype),
        compiler_params=_params("parallel"),
        name="rmsnorm",
    )(x, _row(g))


def _residual_norm_kernel(y_ref, x_ref, gp_ref, gn_ref, xo_ref, ho_ref):
    y = y_ref[...].astype(F32)
    yn = y * lax.rsqrt(jnp.mean(y * y, axis=-1, keepdims=True) + EPS) * gp_ref[...]
    xn = x_ref[...] + yn
    xo_ref[...] = xn
    ho_ref[...] = (xn * lax.rsqrt(jnp.mean(xn * xn, axis=-1, keepdims=True) + EPS)
                   * gn_ref[...]).astype(ho_ref.dtype)


def _residual_only_kernel(y_ref, x_ref, gp_ref, xo_ref):
    y = y_ref[...].astype(F32)
    yn = y * lax.rsqrt(jnp.mean(y * y, axis=-1, keepdims=True) + EPS) * gp_ref[...]
    xo_ref[...] = x_ref[...] + yn


def residual_norm(y, x, g_post, g_next):
    s, d = x.shape
    tm = _tile(s, 256)
    blk = pl.BlockSpec((tm, d), lambda i: (i, 0))
    vec = pl.BlockSpec((1, d), lambda i: (0, 0))
    if g_next is None:
        return pl.pallas_call(
            _residual_only_kernel,
            grid=(s // tm,),
            in_specs=[blk, blk, vec],
            out_specs=blk,
            out_shape=jax.ShapeDtypeStruct((s, d), F32),
            compiler_params=_params("parallel"),
            name="residual_final",
        )(y, x, _row(g_post)), None
    return pl.pallas_call(
        _residual_norm_kernel,
        grid=(s // tm,),
        in_specs=[blk, blk, vec, vec],
        out_specs=[blk, blk],
        out_shape=[jax.ShapeDtypeStruct((s, d), F32), jax.ShapeDtypeStruct((s, d), BF16)],
        compiler_params=_params("parallel"),
        name="residual_norm",
    )(y, x, _row(g_post), _row(g_next))


def _mm_kernel(a_ref, w_ref, o_ref):
    o_ref[...] = jnp.dot(a_ref[...], w_ref[...], preferred_element_type=F32).astype(o_ref.dtype)


def _mm_acc_kernel(a_ref, w_ref, o_ref, acc_ref):
    k = pl.program_id(2)

    @pl.when(k == 0)
    def _():
        acc_ref[...] = jnp.zeros_like(acc_ref)

    acc_ref[...] += jnp.dot(a_ref[...], w_ref[...], preferred_element_type=F32)

    @pl.when(k == pl.num_programs(2) - 1)
    def _():
        o_ref[...] = acc_ref[...].astype(o_ref.dtype)


def matmul(a, w, *, tm=1024, tn=1024, tk=None, out_dtype=BF16, name="matmul"):
    m, kdim = a.shape
    n = w.shape[1]
    tm = _tile(m, tm)
    tn = _tile(n, tn)
    if tk is None or tk >= kdim:
        return pl.pallas_call(
            _mm_kernel,
            grid=(m // tm, n // tn),
            in_specs=[pl.BlockSpec((tm, kdim), lambda i, j: (i, 0)),
                      pl.BlockSpec((kdim, tn), lambda i, j: (0, j))],
            out_specs=pl.BlockSpec((tm, tn), lambda i, j: (i, j)),
            out_shape=jax.ShapeDtypeStruct((m, n), out_dtype),
            compiler_params=_params("parallel", "parallel"),
            name=name,
        )(a, w)
    assert kdim % tk == 0
    return pl.pallas_call(
        _mm_acc_kernel,
        grid=(m // tm, n // tn, kdim // tk),
        in_specs=[pl.BlockSpec((tm, tk), lambda i, j, k: (i, k)),
                  pl.BlockSpec((tk, tn), lambda i, j, k: (k, j))],
        out_specs=pl.BlockSpec((tm, tn), lambda i, j, k: (i, j)),
        out_shape=jax.ShapeDtypeStruct((m, n), out_dtype),
        scratch_shapes=[pltpu.VMEM((tm, tn), F32)],
        compiler_params=_params("parallel", "parallel", "arbitrary"),
        name=name,
    )(a, w)


def _dft_split(s):
    a = 1 << ((s.bit_length() - 1 + 1) // 2)
    assert s % a == 0
    return a, s // a


def _fourier_tables(s):
    a, b = _dft_split(s)
    two_pi = 2.0 * math.pi

    def cs(idx, period):
        ang = (idx % period).astype(F32) * (two_pi / period)
        return jnp.cos(ang), jnp.sin(ang)

    k1 = jnp.arange(a, dtype=jnp.int32)
    c1, s1 = cs(k1[:, None] * k1[None, :], a)
    w1 = jnp.concatenate([c1, -s1], axis=0) * (1.0 / math.sqrt(s))
    k2 = jnp.arange(b, dtype=jnp.int32)
    kk = k1[:, None, None] + a * k2[None, :, None]
    c2, s2 = cs(kk * k2[None, None, :], s)
    g = jnp.concatenate([jnp.concatenate([c2, s2], axis=2),
                         jnp.concatenate([-s2, c2], axis=2)], axis=1)
    c = jnp.arange(F_GROUP_DIM, dtype=jnp.int32)
    cc, sc = cs(c[:, None] * c[None, :], F_GROUP_DIM)
    wc = jnp.concatenate([cc, sc], axis=0) * (1.0 / math.sqrt(F_GROUP_DIM))
    return w1.astype(BF16), g.astype(BF16), wc.astype(BF16)


def _dft1_kernel(w_ref, z_ref, o_ref):
    o_ref[...] = jnp.dot(w_ref[...], z_ref[...], preferred_element_type=F32).astype(o_ref.dtype)


def _dft2_kernel(g_ref, t_ref, wc_ref, o_ref, *, nb, bdim):
    xr, xi = [], []
    for j in range(nb):
        t = t_ref[:, j].reshape(2 * bdim, F_WIDTH)
        x = jnp.dot(g_ref[j], t, preferred_element_type=F32)
        xr.append(x[:bdim].astype(BF16))
        xi.append(x[bdim:].astype(BF16))
    xr = jnp.concatenate(xr, axis=0)
    xi = jnp.concatenate(xi, axis=0)
    for grp in range(F_GROUPS):
        sl = slice(grp * F_GROUP_DIM, (grp + 1) * F_GROUP_DIM)
        y = (jnp.dot(xr[:, sl], wc_ref[:F_GROUP_DIM], preferred_element_type=F32)
             + jnp.dot(xi[:, sl], wc_ref[F_GROUP_DIM:], preferred_element_type=F32))
        for j in range(nb):
            o_ref[:, j * F_WIDTH + grp * F_GROUP_DIM: j * F_WIDTH + (grp + 1) * F_GROUP_DIM] = (
                y[j * bdim:(j + 1) * bdim].astype(o_ref.dtype))


def fourier_mix(zf, tables):
    s = zf.shape[0]
    a, b = _dft_split(s)
    w1, g, wc = tables
    cols = b * F_WIDTH
    tn = _tile(cols, 8 * F_WIDTH)
    t = pl.pallas_call(
        _dft1_kernel,
        grid=(cols // tn,),
        in_specs=[pl.BlockSpec((2 * a, a), lambda j: (0, 0)),
                  pl.BlockSpec((a, tn), lambda j: (0, j))],
        out_specs=pl.BlockSpec((2 * a, tn), lambda j: (0, j)),
        out_shape=jax.ShapeDtypeStruct((2 * a, cols), BF16),
        compiler_params=_params("parallel"),
        name="dft_stage1",
    )(w1, zf.reshape(a, cols))
    nb = _tile(a, 8)
    out = pl.pallas_call(
        functools.partial(_dft2_kernel, nb=nb, bdim=b),
        grid=(a // nb,),
        in_specs=[pl.BlockSpec((nb, 2 * b, 2 * b), lambda i: (i, 0, 0)),
                  pl.BlockSpec((2, nb, b, F_WIDTH), lambda i: (0, i, 0, 0)),
                  pl.BlockSpec((2 * F_GROUP_DIM, F_GROUP_DIM), lambda i: (0, 0))],
        out_specs=pl.BlockSpec((b, nb * F_WIDTH), lambda i: (0, i)),
        out_shape=jax.ShapeDtypeStruct((b, a * F_WIDTH), BF16),
        compiler_params=_params("parallel"),
        name="dft_stage2",
    )(g, t.reshape(2, a, b, F_WIDTH), wc)
    return out.reshape(s, F_WIDTH)


def _rope_lanes(y2, cs):
    w = y2 * cs
    return w + pltpu.roll(w, QK_ROPE, axis=1)


def _mla_q_kernel(cq_ref, gq_ref, cs_ref, w_ref, q_ref, n_ref, *, scale):
    @pl.when(pl.program_id(1) == 0)
    def _():
        c = cq_ref[...].astype(F32)
        n_ref[...] = (c * lax.rsqrt(jnp.mean(c * c, axis=-1, keepdims=True) + EPS)
                      * gq_ref[...]).astype(BF16)

    y = jnp.dot(n_ref[...], w_ref[...], preferred_element_type=F32) * scale
    q_ref[:, :QK_NOPE] = y[:, :QK_NOPE].astype(q_ref.dtype)
    q_ref[:, QK_NOPE:] = _rope_lanes(y[:, QK_NOPE:], cs_ref[...]).astype(q_ref.dtype)


def _mla_kv_kernel(ckv_ref, kr_ref, gkv_ref, cs_ref, w_ref, k_ref, v_ref, n_ref, r_ref):
    @pl.when(pl.program_id(1) == 0)
    def _():
        c = ckv_ref[...].astype(F32)
        n_ref[...] = (c * lax.rsqrt(jnp.mean(c * c, axis=-1, keepdims=True) + EPS)
                      * gkv_ref[...]).astype(BF16)
        r = _rope_lanes(kr_ref[...].astype(F32), cs_ref[...])
        lane = lax.broadcasted_iota(jnp.int32, r.shape, 1)
        r_ref[...] = jnp.where(lane < QK_ROPE, r, 0.0).astype(BF16)

    y = jnp.dot(n_ref[...], w_ref[...], preferred_element_type=F32)
    k_ref[:, :QK_NOPE] = y[:, :QK_NOPE].astype(k_ref.dtype)
    k_ref[:, QK_NOPE:] = r_ref[...]
    v_ref[...] = y[:, QK_NOPE:].astype(v_ref.dtype)


def _flash_kernel(q_ref, k_ref, v_ref, o_ref, m_ref, l_ref, acc_ref, *, tk):
    nk = k_ref.shape[0] // tk
    m_ref[...] = jnp.full_like(m_ref, -jnp.inf)
    l_ref[...] = jnp.zeros_like(l_ref)
    acc_ref[...] = jnp.zeros_like(acc_ref)
    q = q_ref[...]

    def body(c, carry):
        off = pl.multiple_of(c * tk, tk)
        k = k_ref[pl.ds(off, tk), :]
        v = v_ref[pl.ds(off, tk), :]
        s = lax.dot_general(q, k, (((1,), (1,)), ((), ())), preferred_element_type=F32)
        m_prev = m_ref[...]
        m_new = jnp.maximum(m_prev, jnp.max(s, axis=-1, keepdims=True))
        alpha = jnp.exp(m_prev - m_new)
        p = jnp.exp(s - m_new)
        l_ref[...] = alpha * l_ref[...] + jnp.sum(p, axis=-1, keepdims=True)
        acc_ref[...] = alpha * acc_ref[...] + jnp.dot(p.astype(BF16), v, preferred_element_type=F32)
        m_ref[...] = m_new
        return carry

    lax.fori_loop(0, nk, body, 0)
    o_ref[...] = (acc_ref[...] / l_ref[...]).astype(o_ref.dtype)


def mla_mix(cq, ckv, kr2, cs, g_q, wq, g_kv, wkv):
    s = cq.shape[0]
    tm = _tile(s, 1024)
    scale = 1.0 / math.sqrt(QK_NOPE + QK_ROPE)
    dq = QK_NOPE + 2 * QK_ROPE
    q = pl.pallas_call(
        functools.partial(_mla_q_kernel, scale=scale),
        grid=(s // tm, MLA_HEADS),
        in_specs=[pl.BlockSpec((tm, Q_LORA), lambda i, h: (i, 0)),
                  pl.BlockSpec((1, Q_LORA), lambda i, h: (0, 0)),
                  pl.BlockSpec((tm, LANES), lambda i, h: (i, 0)),
                  pl.BlockSpec((None, Q_LORA, dq), lambda i, h: (h, 0, 0))],
        out_specs=pl.BlockSpec((None, tm, dq), lambda i, h: (h, i, 0)),
        out_shape=jax.ShapeDtypeStruct((MLA_HEADS, s, dq), BF16),
        scratch_shapes=[pltpu.VMEM((tm, Q_LORA), BF16)],
        compiler_params=_params("parallel", "arbitrary"),
        name="mla_q",
    )(cq, _row(g_q), cs, wq)
    k, v = pl.pallas_call(
        _mla_kv_kernel,
        grid=(s // tm, MLA_HEADS),
        in_specs=[pl.BlockSpec((tm, KV_LORA), lambda i, h: (i, 0)),
                  pl.BlockSpec((tm, LANES), lambda i, h: (i, 0)),
                  pl.BlockSpec((1, KV_LORA), lambda i, h: (0, 0)),
                  pl.BlockSpec((tm, LANES), lambda i, h: (i, 0)),
                  pl.BlockSpec((None, KV_LORA, QK_NOPE + V_DIM), lambda i, h: (h, 0, 0))],
        out_specs=[pl.BlockSpec((None, tm, dq), lambda i, h: (h, i, 0)),
                   pl.BlockSpec((None, tm, V_DIM), lambda i, h: (h, i, 0))],
        out_shape=[jax.ShapeDtypeStruct((MLA_HEADS, s, dq), BF16),
                   jax.ShapeDtypeStruct((MLA_HEADS, s, V_DIM), BF16)],
        scratch_shapes=[pltpu.VMEM((tm, KV_LORA), BF16), pltpu.VMEM((tm, LANES), BF16)],
        compiler_params=_params("parallel", "arbitrary"),
        name="mla_kv",
    )(ckv, kr2, _row(g_kv), cs, wkv)
    tq = _tile(s, 512)
    tk = _tile(s, 1024)
    return pl.pallas_call(
        functools.partial(_flash_kernel, tk=tk),
        grid=(MLA_HEADS, s // tq),
        in_specs=[pl.BlockSpec((None, tq, dq), lambda h, i: (h, i, 0)),
                  pl.BlockSpec((None, s, dq), lambda h, i: (h, 0, 0)),
                  pl.BlockSpec((None, s, V_DIM), lambda h, i: (h, 0, 0))],
        out_specs=pl.BlockSpec((tq, V_DIM), lambda h, i: (i, h)),
        out_shape=jax.ShapeDtypeStruct((s, MLA_HEADS * V_DIM), BF16),
        scratch_shapes=[pltpu.VMEM((tq, 1), F32), pltpu.VMEM((tq, 1), F32),
                        pltpu.VMEM((tq, V_DIM), F32)],
        compiler_params=_params("parallel", "arbitrary"),
        name="mla_flash",
    )(q, k, v)


def _gelu(x):
    return jax.nn.gelu(x, approximate=True)


def _sg_kernel(u_ref, v_ref, g_ref, w_ref, b_ref, o_ref):
    v = _gelu(v_ref[...].astype(F32))
    mu = jnp.mean(v, axis=-1, keepdims=True)
    vc = v - mu
    vn = (vc * lax.rsqrt(jnp.mean(vc * vc, axis=-1, keepdims=True) + EPS) * g_ref[...]).astype(BF16)
    w = w_ref[...]
    b = b_ref[...]
    for c in range(u_ref.shape[0] // CHUNK):
        rows = slice(c * CHUNK, (c + 1) * CHUNK)
        sp = jnp.dot(w, vn[rows], preferred_element_type=F32) + b
        o_ref[rows, :] = (_gelu(u_ref[rows, :].astype(F32)) * sp).astype(o_ref.dtype)


def spatial_gating(zu, zv, ln_gain, w_spatial, b_spatial):
    s = zu.shape[0]
    tm = _tile(s, 1024)
    blk = pl.BlockSpec((tm, SG_GROUP_DIM), lambda i, g: (i, g))
    return pl.pallas_call(
        _sg_kernel,
        grid=(s // tm, SG_GROUPS),
        in_specs=[blk, blk,
                  pl.BlockSpec((1, SG_GROUP_DIM), lambda i, g: (0, g)),
                  pl.BlockSpec((None, CHUNK, CHUNK), lambda i, g: (g, 0, 0)),
                  pl.BlockSpec((None, CHUNK, 1), lambda i, g: (g, 0, 0))],
        out_specs=blk,
        out_shape=jax.ShapeDtypeStruct((s, SG_WIDTH), BF16),
        compiler_params=_params("parallel", "parallel"),
        name="spatial_gating",
    )(zu, zv, _row(ln_gain), w_spatial.astype(BF16), b_spatial.astype(F32)[:, :, None])


def _merge_kernel(h_ref, f_ref, a_ref, s_ref, wg0_ref, wg1_ref, wg2_ref, b0_ref, b1_ref, b2_ref,
                  wf_ref, wa_ref, ws_ref, o_ref):
    h = h_ref[...]

    def gated(wg_ref, b_ref, y_ref, w_ref):
        gate = jax.nn.sigmoid(jnp.dot(h, wg_ref[...], preferred_element_type=F32) + b_ref[...])
        return gate * jnp.dot(y_ref[...], w_ref[...], preferred_element_type=F32)

    o_ref[...] = (gated(wg0_ref, b0_ref, f_ref, wf_ref) + gated(wg1_ref, b1_ref, a_ref, wa_ref)
                  + gated(wg2_ref, b2_ref, s_ref, ws_ref)).astype(o_ref.dtype)


def gated_merge(h, yf, ya, ys, w_gate, b_gate, w_f, w_a, w_s):
    s, d = h.shape
    tm = _tile(s, 512)
    tn = _tile(d, 256)
    nj = d // tn
    bg = _row(b_gate)

    def rows(width):
        return pl.BlockSpec((tm, width), lambda i, j: (i, 0))

    def gcol(b):
        return pl.BlockSpec((d, tn), lambda i, j, b=b: (0, j + b * nj))

    def bcol(b):
        return pl.BlockSpec((1, tn), lambda i, j, b=b: (0, j + b * nj))

    def wcol(width):
        return pl.BlockSpec((width, tn), lambda i, j: (0, j))

    return pl.pallas_call(
        _merge_kernel,
        grid=(s // tm, nj),
        in_specs=[rows(d), rows(F_WIDTH), rows(MLA_HEADS * V_DIM), rows(SG_WIDTH),
                  gcol(0), gcol(1), gcol(2), bcol(0), bcol(1), bcol(2),
                  wcol(F_WIDTH), wcol(MLA_HEADS * V_DIM), wcol(SG_WIDTH)],
        out_specs=pl.BlockSpec((tm, tn), lambda i, j: (i, j)),
        out_shape=jax.ShapeDtypeStruct((s, d), BF16),
        compiler_params=_params("parallel", "parallel"),
        name="gated_merge",
    )(h, yf, ya, ys, w_gate, w_gate, w_gate, bg, bg, bg, w_f, w_a, w_s)


def _cross_attn_kernel(q_ref, k_ref, v_ref, o_ref, *, scale):
    for hd in range(MEM_HEADS):
        sl = slice(hd * MEM_HEAD_DIM, (hd + 1) * MEM_HEAD_DIM)
        s = lax.dot_general(q_ref[:, sl], k_ref[:, sl], (((1,), (1,)), ((), ())),
                            preferred_element_type=F32) * scale
        p = jnp.exp(s - jnp.max(s, axis=-1, keepdims=True))
        o = jnp.dot(p.astype(BF16), v_ref[:, sl], preferred_element_type=F32)
        o_ref[:, sl] = (o / jnp.sum(p, axis=-1, keepdims=True)).astype(o_ref.dtype)


def cross_attention(q, k, v):
    s, width = q.shape
    m = k.shape[0]
    tm = _tile(s, 1024)
    return pl.pallas_call(
        functools.partial(_cross_attn_kernel, scale=1.0 / math.sqrt(MEM_HEAD_DIM)),
        grid=(s // tm,),
        in_specs=[pl.BlockSpec((tm, width), lambda i: (i, 0)),
                  pl.BlockSpec((m, width), lambda i: (0, 0)),
                  pl.BlockSpec((m, width), lambda i: (0, 0))],
        out_specs=pl.BlockSpec((tm, width), lambda i: (i, 0)),
        out_shape=jax.ShapeDtypeStruct((s, width), BF16),
        compiler_params=_params("parallel"),
        name="cross_attention",
    )(q, k, v)


def _ffn_up_kernel(h_ref, hp_ref, hn_ref, wg_ref, wv_ref, cwg_ref, cwv_ref, cbg_ref, cbv_ref, o_ref):
    i = pl.program_id(0)
    tm = h_ref.shape[0]
    h = h_ref[...]
    hp = jnp.where(i > 0, hp_ref[...], jnp.zeros_like(hp_ref))
    hn = jnp.where(i < pl.num_programs(0) - 1, hn_ref[...], jnp.zeros_like(hn_ref))
    row = lax.broadcasted_iota(jnp.int32, (tm, 1), 0)

    def conv(w_ref, cw_ref, cb_ref):
        w = w_ref[...]
        u = jnp.dot(h, w, preferred_element_type=F32)
        up = jnp.dot(hp, w, preferred_element_type=F32)[7:8]
        un = jnp.dot(hn, w, preferred_element_type=F32)[0:1]
        above = jnp.where(row == 0, up, pltpu.roll(u, 1, axis=0))
        below = jnp.where(row == tm - 1, un, pltpu.roll(u, tm - 1, axis=0))
        cw = cw_ref[...]
        return above * cw[0:1] + u * cw[1:2] + below * cw[2:3] + cb_ref[...]

    o_ref[...] = (_gelu(conv(wg_ref, cwg_ref, cbg_ref)) * conv(wv_ref, cwv_ref, cbv_ref)).astype(o_ref.dtype)


def ffn_up(h, w_up, conv_w, conv_b):
    s, d = h.shape
    d_ff = w_up.shape[1] // 2
    tm = _tile(s, 1024)
    tn = _tile(d_ff, 256)
    nj = d_ff // tn
    hb = tm // 8
    last = s // 8 - 1
    cb = _row(conv_b)
    cw = conv_w.astype(F32)
    return pl.pallas_call(
        _ffn_up_kernel,
        grid=(s // tm, nj),
        in_specs=[pl.BlockSpec((tm, d), lambda i, j: (i, 0)),
                  pl.BlockSpec((8, d), lambda i, j: (jnp.maximum(i * hb - 1, 0), 0)),
                  pl.BlockSpec((8, d), lambda i, j: (jnp.minimum((i + 1) * hb, last), 0)),
                  pl.BlockSpec((d, tn), lambda i, j: (0, j)),
                  pl.BlockSpec((d, tn), lambda i, j: (0, j + nj)),
                  pl.BlockSpec((3, tn), lambda i, j: (0, j)),
                  pl.BlockSpec((3, tn), lambda i, j: (0, j + nj)),
                  pl.BlockSpec((1, tn), lambda i, j: (0, j)),
                  pl.BlockSpec((1, tn), lambda i, j: (0, j + nj))],
        out_specs=pl.BlockSpec((tm, tn), lambda i, j: (i, j)),
        out_shape=jax.ShapeDtypeStruct((s, d_ff), BF16),
        compiler_params=_params("parallel", "parallel"),
        name="ffn_up_conv",
    )(h, h, h, w_up, w_up, cw, cw, cb, cb)


def _split_w_in(w_in):
    half = QK_ROPE // 2
    w_kr = w_in[:, OFF_KR:OFF_SG]
    w_kr_rot = jnp.concatenate([-w_kr[:, half:], w_kr[:, :half]], axis=1)
    return (w_in[:, :OFF_CQ], w_in[:, OFF_CQ:OFF_CKV], w_in[:, OFF_CKV:OFF_KR],
            jnp.concatenate([w_kr, w_kr_rot], axis=1),
            w_in[:, OFF_SG:OFF_SG + SG_WIDTH], w_in[:, OFF_SG + SG_WIDTH:])


def _prep_w_uq(w_uq):
    half = QK_ROPE // 2
    w = w_uq.reshape(Q_LORA, MLA_HEADS, QK_NOPE + QK_ROPE)
    rope = w[:, :, QK_NOPE:]
    rot = jnp.concatenate([-rope[:, :, half:], rope[:, :, :half]], axis=2)
    return jnp.concatenate([w, rot], axis=2).transpose(1, 0, 2).astype(BF16)


def _prep_w_ukv(w_ukv):
    return w_ukv.reshape(KV_LORA, MLA_HEADS, QK_NOPE + V_DIM).transpose(1, 0, 2).astype(BF16)


def kernel(x, mem, positions, mix_pre_norm, mix_post_norm, w_in, mla_q_norm, w_uq, mla_kv_norm, w_ukv, sg_norm, w_spatial, b_spatial, w_br_f, w_br_a, w_br_s, w_gate, b_gate, w_out, mem_pre_norm, mem_post_norm, mem_kv_norm, w_mq, w_mk, w_mv, w_mo, ffn_pre_norm, ffn_post_norm, w_up, conv_w, conv_b, w_down):
    batch, s, d = x.shape
    depth = w_in.shape[0]
    d_ff = w_down.shape[1]
    tk_down = d_ff // 2 if (d_ff // 2) % LANES == 0 else None

    inv_freq = ROPE_THETA ** (-jnp.arange(0, QK_ROPE, 2, dtype=F32) / QK_ROPE)
    tables = _fourier_tables(s)
    outs = []
    for b in range(batch):
        ang = positions[b].astype(F32)[:, None] * inv_freq
        cos, sin = jnp.cos(ang), jnp.sin(ang)
        cs = jnp.concatenate([cos, cos, sin, sin], axis=1)
        xb = x[b]
        mem_b = mem[b]
        h = rmsnorm(xb, mix_pre_norm[0])
        for l in range(depth):
            w_f, w_cq, w_ckv, w_kr2, w_su, w_sv = [w.astype(BF16) for w in _split_w_in(w_in[l])]
            zf = matmul(h, w_f, name="in_fourier")
            cq = matmul(h, w_cq, name="in_cq")
            ckv = matmul(h, w_ckv, name="in_ckv")
            kr2 = matmul(h, w_kr2, name="in_krope")
            zu = matmul(h, w_su, name="in_sg_u")
            zv = matmul(h, w_sv, name="in_sg_v")
            yf = fourier_mix(zf, tables)
            ya = mla_mix(cq, ckv, kr2, cs, mla_q_norm[l], _prep_w_uq(w_uq[l]),
                         mla_kv_norm[l], _prep_w_ukv(w_ukv[l]))
            ys = spatial_gating(zu, zv, sg_norm[l], w_spatial[l], b_spatial[l])
            merged = gated_merge(h, yf, ya, ys, w_gate[l].astype(BF16), b_gate[l],
                                 w_br_f[l].astype(BF16), w_br_a[l].astype(BF16), w_br_s[l].astype(BF16))
            y = matmul(merged, w_out[l].astype(BF16), name="mix_out")
            xb, h = residual_norm(y, xb, mix_post_norm[l], mem_pre_norm[l])

            mem_n = rmsnorm(mem_b, mem_kv_norm[l])
            q = matmul(h, w_mq[l].astype(BF16), name="mem_q")
            k = matmul(mem_n, w_mk[l].astype(BF16), name="mem_k")
            v = matmul(mem_n, w_mv[l].astype(BF16), name="mem_v")
            o = cross_attention(q, k, v)
            y = matmul(o, w_mo[l].astype(BF16), name="mem_out")
            xb, h = residual_norm(y, xb, mem_post_norm[l], ffn_pre_norm[l])

            act = ffn_up(h, w_up[l].astype(BF16), conv_w[l], conv_b[l])
            y = matmul(act, w_down[l].astype(BF16), tn=512, tk=tk_down, name="ffn_down")
            g_next = mix_pre_norm[l + 1] if l + 1 < depth else None
            xb, h = residual_norm(y, xb, ffn_post_norm[l], g_next)
        outs.append(xb)
    return jnp.stack(outs, axis=0)
```

```python
import functools
import math

import jax
import jax.numpy as jnp
from jax import lax
from jax.experimental import pallas as pl
from jax.experimental.pallas import tpu as pltpu

F32 = jnp.float32
BF16 = jnp.bfloat16
EPS = 1e-6

V7X_VMEM_BYTES = 64 * 1024 * 1024
VMEM_LIMIT_BYTES = V7X_VMEM_BYTES - 4 * 1024 * 1024
LANES = 128

F_GROUPS = 4
F_GROUP_DIM = 256
F_WIDTH = F_GROUPS * F_GROUP_DIM
MLA_HEADS = 8
QK_NOPE = 128
QK_ROPE = 64
V_DIM = 128
Q_LORA = 768
KV_LORA = 512
ROPE_THETA = 10000.0
SG_GROUPS = 16
SG_GROUP_DIM = 128
SG_WIDTH = SG_GROUPS * SG_GROUP_DIM
CHUNK = 128
MEM_HEADS = 4
MEM_HEAD_DIM = 256
OFF_CQ = F_WIDTH
OFF_CKV = OFF_CQ + Q_LORA
OFF_KR = OFF_CKV + KV_LORA
OFF_SG = OFF_KR + QK_ROPE


def _params(*sem):
    return pltpu.CompilerParams(dimension_semantics=sem, vmem_limit_bytes=VMEM_LIMIT_BYTES)


def _tile(n, pref):
    if n <= pref:
        return n
    t = pref
    while n % t:
        t //= 2
    return t


def _row(v):
    return v.reshape(1, -1).astype(F32)


def _rmsnorm_kernel(x_ref, g_ref, o_ref):
    x = x_ref[...].astype(F32)
    ms = jnp.mean(x * x, axis=-1, keepdims=True)
    o_ref[...] = (x * lax.rsqrt(ms + EPS) * g_ref[...]).astype(o_ref.dtype)


def rmsnorm(x, g, out_dtype=BF16):
    s, d = x.shape
    tm = _tile(s, 512)
    return pl.pallas_call(
        _rmsnorm_kernel,
        grid=(s // tm,),
        in_specs=[pl.BlockSpec((tm, d), lambda i: (i, 0)), pl.BlockSpec((1, d), lambda i: (0, 0))],
        out_specs=pl.BlockSpec((tm, d), lambda i: (i, 0)),
        out_shape=jax.ShapeDtypeStruct((s, d), out_dtype),
        compiler_params=_params("parallel"),
        name="rmsnorm",
    )(x, _row(g))


def _residual_norm_kernel(y_ref, x_ref, gp_ref, gn_ref, xo_ref, ho_ref):
    y = y_ref[...].astype(F32)
    yn = y * lax.rsqrt(jnp.mean(y * y, axis=-1, keepdims=True) + EPS) * gp_ref[...]
    xn = x_ref[...] + yn
    xo_ref[...] = xn
    ho_ref[...] = (xn * lax.rsqrt(jnp.mean(xn * xn, axis=-1, keepdims=True) + EPS)
                   * gn_ref[...]).astype(ho_ref.dtype)


def _residual_only_kernel(y_ref, x_ref, gp_ref, xo_ref):
    y = y_ref[...].astype(F32)
    yn = y * lax.rsqrt(jnp.mean(y * y, axis=-1, keepdims=True) + EPS) * gp_ref[...]
    xo_ref[...] = x_ref[...] + yn


def residual_norm(y, x, g_post, g_next):
    s, d = x.shape
    tm = _tile(s, 256)
    blk = pl.BlockSpec((tm, d), lambda i: (i, 0))
    vec = pl.BlockSpec((1, d), lambda i: (0, 0))
    if g_next is None:
        return pl.pallas_call(
            _residual_only_kernel,
            grid=(s // tm,),
            in_specs=[blk, blk, vec],
            out_specs=blk,
            out_shape=jax.ShapeDtypeStruct((s, d), F32),
            compiler_params=_params("parallel"),
            name="residual_final",
        )(y, x, _row(g_post)), None
    return pl.pallas_call(
        _residual_norm_kernel,
        grid=(s // tm,),
        in_specs=[blk, blk, vec, vec],
        out_specs=[blk, blk],
        out_shape=[jax.ShapeDtypeStruct((s, d), F32), jax.ShapeDtypeStruct((s, d), BF16)],
        compiler_params=_params("parallel"),
        name="residual_norm",
    )(y, x, _row(g_post), _row(g_next))


def _mm_kernel(a_ref, w_ref, o_ref):
    o_ref[...] = jnp.dot(a_ref[...], w_ref[...], preferred_element_type=F32).astype(o_ref.dtype)


def _mm_acc_kernel(a_ref, w_ref, o_ref, acc_ref):
    k = pl.program_id(2)

    @pl.when(k == 0)
    def _():
        acc_ref[...] = jnp.zeros_like(acc_ref)

    acc_ref[...] += jnp.dot(a_ref[...], w_ref[...], preferred_element_type=F32)

    @pl.when(k == pl.num_programs(2) - 1)
    def _():
        o_ref[...] = acc_ref[...].astype(o_ref.dtype)


def matmul(a, w, *, tm=1024, tn=1024, tk=None, out_dtype=BF16, name="matmul"):
    m, kdim = a.shape
    n = w.shape[1]
    tm = _tile(m, tm)
    tn = _tile(n, tn)
    if tk is None or tk >= kdim:
        return pl.pallas_call(
            _mm_kernel,
            grid=(m // tm, n // tn),
            in_specs=[pl.BlockSpec((tm, kdim), lambda i, j: (i, 0)),
                      pl.BlockSpec((kdim, tn), lambda i, j: (0, j))],
            out_specs=pl.BlockSpec((tm, tn), lambda i, j: (i, j)),
            out_shape=jax.ShapeDtypeStruct((m, n), out_dtype),
            compiler_params=_params("parallel", "parallel"),
            name=name,
        )(a, w)
    assert kdim % tk == 0
    return pl.pallas_call(
        _mm_acc_kernel,
        grid=(m // tm, n // tn, kdim // tk),
        in_specs=[pl.BlockSpec((tm, tk), lambda i, j, k: (i, k)),
                  pl.BlockSpec((tk, tn), lambda i, j, k: (k, j))],
        out_specs=pl.BlockSpec((tm, tn), lambda i, j, k: (i, j)),
        out_shape=jax.ShapeDtypeStruct((m, n), out_dtype),
        scratch_shapes=[pltpu.VMEM((tm, tn), F32)],
        compiler_params=_params("parallel", "parallel", "arbitrary"),
        name=name,
    )(a, w)


def _dft_split(s):
    a = 1 << ((s.bit_length() - 1 + 1) // 2)
    assert s % a == 0
    return a, s // a


def _fourier_tables(s):
    a, b = _dft_split(s)
    two_pi = 2.0 * math.pi

    def cs(idx, period):
        ang = (idx % period).astype(F32) * (two_pi / period)
        return jnp.cos(ang), jnp.sin(ang)

    k1 = jnp.arange(a, dtype=jnp.int32)
    c1, s1 = cs(k1[:, None] * k1[None, :], a)
    w1 = jnp.concatenate([c1, -s1], axis=0) * (1.0 / math.sqrt(s))
    k2 = jnp.arange(b, dtype=jnp.int32)
    kk = k1[:, None, None] + a * k2[None, :, None]
    c2, s2 = cs(kk * k2[None, None, :], s)
    g = jnp.concatenate([jnp.concatenate([c2, s2], axis=2),
                         jnp.concatenate([-s2, c2], axis=2)], axis=1)
    c = jnp.arange(F_GROUP_DIM, dtype=jnp.int32)
    cc, sc = cs(c[:, None] * c[None, :], F_GROUP_DIM)
    wc = jnp.concatenate([cc, sc], axis=0) * (1.0 / math.sqrt(F_GROUP_DIM))
    return w1.astype(BF16), g.astype(BF16), wc.astype(BF16)


def _dft1_kernel(w_ref, z_ref, o_ref):
    o_ref[...] = jnp.dot(w_ref[...], z_ref[...], preferred_element_type=F32).astype(o_ref.dtype)


def _dft2_kernel(g_ref, t_ref, wc_ref, o_ref, *, nb, bdim):
    xr, xi = [], []
    for j in range(nb):
        t = t_ref[:, j].reshape(2 * bdim, F_WIDTH)
        x = jnp.dot(g_ref[j], t, preferred_element_type=F32)
        xr.append(x[:bdim].astype(BF16))
        xi.append(x[bdim:].astype(BF16))
    xr = jnp.concatenate(xr, axis=0)
    xi = jnp.concatenate(xi, axis=0)
    for grp in range(F_GROUPS):
        sl = slice(grp * F_GROUP_DIM, (grp + 1) * F_GROUP_DIM)
        y = (jnp.dot(xr[:, sl], wc_ref[:F_GROUP_DIM], preferred_element_type=F32)
             + jnp.dot(xi[:, sl], wc_ref[F_GROUP_DIM:], preferred_element_type=F32))
        for j in range(nb):
            o_ref[:, j * F_WIDTH + grp * F_GROUP_DIM: j * F_WIDTH + (grp + 1) * F_GROUP_DIM] = (
                y[j * bdim:(j + 1) * bdim].astype(o_ref.dtype))


def fourier_mix(zf, tables):
    s = zf.shape[0]
    a, b = _dft_split(s)
    w1, g, wc = tables
    cols = b * F_WIDTH
    tn = _tile(cols, 8 * F_WIDTH)
    t = pl.pallas_call(
        _dft1_kernel,
        grid=(cols // tn,),
        in_specs=[pl.BlockSpec((2 * a, a), lambda j: (0, 0)),
                  pl.BlockSpec((a, tn), lambda j: (0, j))],
        out_specs=pl.BlockSpec((2 * a, tn), lambda j: (0, j)),
        out_shape=jax.ShapeDtypeStruct((2 * a, cols), BF16),
        compiler_params=_params("parallel"),
        name="dft_stage1",
    )(w1, zf.reshape(a, cols))
    nb = _tile(a, 8)
    out = pl.pallas_call(
        functools.partial(_dft2_kernel, nb=nb, bdim=b),
        grid=(a // nb,),
        in_specs=[pl.BlockSpec((nb, 2 * b, 2 * b), lambda i: (i, 0, 0)),
                  pl.BlockSpec((2, nb, b, F_WIDTH), lambda i: (0, i, 0, 0)),
                  pl.BlockSpec((2 * F_GROUP_DIM, F_GROUP_DIM), lambda i: (0, 0))],
        out_specs=pl.BlockSpec((b, nb * F_WIDTH), lambda i: (0, i)),
        out_shape=jax.ShapeDtypeStruct((b, a * F_WIDTH), BF16),
        compiler_params=_params("parallel"),
        name="dft_stage2",
    )(g, t.reshape(2, a, b, F_WIDTH), wc)
    return out.reshape(s, F_WIDTH)


def _rope_lanes(y2, cs):
    w = y2 * cs
    return w + pltpu.roll(w, QK_ROPE, axis=1)


def _mla_q_kernel(cq_ref, gq_ref, cs_ref, w_ref, q_ref, n_ref, *, scale):
    @pl.when(pl.program_id(1) == 0)
    def _():
        c = cq_ref[...].astype(F32)
        n_ref[...] = (c * lax.rsqrt(jnp.mean(c * c, axis=-1, keepdims=True) + EPS)
                      * gq_ref[...]).astype(BF16)

    y = jnp.dot(n_ref[...], w_ref[...], preferred_element_type=F32) * scale
    q_ref[:, :QK_NOPE] = y[:, :QK_NOPE].astype(q_ref.dtype)
    q_ref[:, QK_NOPE:] = _rope_lanes(y[:, QK_NOPE:], cs_ref[...]).astype(q_ref.dtype)


def _mla_kv_kernel(ckv_ref, kr_ref, gkv_ref, cs_ref, w_ref, k_ref, v_ref, n_ref, r_ref):
    @pl.when(pl.program_id(1) == 0)
    def _():
        c = ckv_ref[...].astype(F32)
        n_ref[...] = (c * lax.rsqrt(jnp.mean(c * c, axis=-1, keepdims=True) + EPS)
                      * gkv_ref[...]).astype(BF16)
        r = _rope_lanes(kr_ref[...].astype(F32), cs_ref[...])
        lane = lax.broadcasted_iota(jnp.int32, r.shape, 1)
        r_ref[...] = jnp.where(lane < QK_ROPE, r, 0.0).astype(BF16)

    y = jnp.dot(n_ref[...], w_ref[...], preferred_element_type=F32)
    k_ref[:, :QK_NOPE] = y[:, :QK_NOPE].astype(k_ref.dtype)
    k_ref[:, QK_NOPE:] = r_ref[...]
    v_ref[:, :V_DIM] = y[:, QK_NOPE:].astype(v_ref.dtype)
    v_ref[:, V_DIM:] = jnp.ones((v_ref.shape[0], V_DIM), v_ref.dtype)


def _flash_kernel(q_ref, k_ref, v_ref, o_ref, sa_ref, sb_ref, m_ref, acc_ref, *, tk):
    nk = k_ref.shape[0] // tk
    q = q_ref[...]

    def scores(c):
        off = pl.multiple_of(c * tk, tk)
        return lax.dot_general(q, k_ref[pl.ds(off, tk), :], (((1,), (1,)), ((), ())),
                               preferred_element_type=F32)

    def accumulate(c, s_ref):
        off = pl.multiple_of(c * tk, tk)
        s = s_ref[...]
        m_prev = m_ref[...]
        m_new = jnp.maximum(m_prev, jnp.max(s, axis=-1, keepdims=True))
        p = jnp.exp2(s - m_new).astype(BF16)
        acc_ref[...] = (jnp.exp2(m_prev - m_new) * acc_ref[...]
                        + jnp.dot(p, v_ref[pl.ds(off, tk), :], preferred_element_type=F32))
        m_ref[...] = m_new

    m_ref[...] = jnp.full_like(m_ref, -jnp.inf)
    acc_ref[...] = jnp.zeros_like(acc_ref)
    sa_ref[...] = scores(0)

    def body(c2, carry):
        a = 2 * c2
        sb_ref[...] = scores(a + 1)
        accumulate(a, sa_ref)
        sa_ref[...] = scores(a + 2)
        accumulate(a + 1, sb_ref)
        return carry

    lax.fori_loop(0, nk // 2 - 1, body, 0)
    sb_ref[...] = scores(nk - 1)
    accumulate(nk - 2, sa_ref)
    accumulate(nk - 1, sb_ref)
    acc = acc_ref[...]
    o_ref[...] = (acc[:, :V_DIM] / acc[:, V_DIM:]).astype(o_ref.dtype)


def mla_mix(cq, ckvr, cs, g_q, wq, g_kv, wkv):
    s = cq.shape[0]
    tm = _tile(s, 1024)
    scale = math.log2(math.e) / math.sqrt(QK_NOPE + QK_ROPE)
    dq = QK_NOPE + 2 * QK_ROPE
    dv = 2 * V_DIM
    q = pl.pallas_call(
        functools.partial(_mla_q_kernel, scale=scale),
        grid=(s // tm, MLA_HEADS),
        in_specs=[pl.BlockSpec((tm, Q_LORA), lambda i, h: (i, 0)),
                  pl.BlockSpec((1, Q_LORA), lambda i, h: (0, 0)),
                  pl.BlockSpec((tm, LANES), lambda i, h: (i, 0)),
                  pl.BlockSpec((None, Q_LORA, dq), lambda i, h: (h, 0, 0))],
        out_specs=pl.BlockSpec((None, tm, dq), lambda i, h: (h, i, 0)),
        out_shape=jax.ShapeDtypeStruct((MLA_HEADS, s, dq), BF16),
        scratch_shapes=[pltpu.VMEM((tm, Q_LORA), BF16)],
        compiler_params=_params("parallel", "arbitrary"),
        name="mla_q",
    )(cq, _row(g_q), cs, wq)
    k, v = pl.pallas_call(
        _mla_kv_kernel,
        grid=(s // tm, MLA_HEADS),
        in_specs=[pl.BlockSpec((tm, KV_LORA), lambda i, h: (i, 0)),
                  pl.BlockSpec((tm, LANES), lambda i, h: (i, KV_LORA // LANES)),
                  pl.BlockSpec((1, KV_LORA), lambda i, h: (0, 0)),
                  pl.BlockSpec((tm, LANES), lambda i, h: (i, 0)),
                  pl.BlockSpec((None, KV_LORA, QK_NOPE + V_DIM), lambda i, h: (h, 0, 0))],
        out_specs=[pl.BlockSpec((None, tm, dq), lambda i, h: (h, i, 0)),
                   pl.BlockSpec((None, tm, dv), lambda i, h: (h, i, 0))],
        out_shape=[jax.ShapeDtypeStruct((MLA_HEADS, s, dq), BF16),
                   jax.ShapeDtypeStruct((MLA_HEADS, s, dv), BF16)],
        scratch_shapes=[pltpu.VMEM((tm, KV_LORA), BF16), pltpu.VMEM((tm, LANES), BF16)],
        compiler_params=_params("parallel", "arbitrary"),
        name="mla_kv",
    )(ckvr, ckvr, _row(g_kv), cs, wkv)
    tq = _tile(s, 1024)
    tk = _tile(s // 2, 1024)
    assert (s // tk) % 2 == 0
    once = pl.Buffered(1)
    return pl.pallas_call(
        functools.partial(_flash_kernel, tk=tk),
        grid=(MLA_HEADS, s // tq),
        in_specs=[pl.BlockSpec((None, tq, dq), lambda h, i: (h, i, 0)),
                  pl.BlockSpec((None, s, dq), lambda h, i: (h, 0, 0), pipeline_mode=once),
                  pl.BlockSpec((None, s, dv), lambda h, i: (h, 0, 0), pipeline_mode=once)],
        out_specs=pl.BlockSpec((tq, V_DIM), lambda h, i: (i, h)),
        out_shape=jax.ShapeDtypeStruct((s, MLA_HEADS * V_DIM), BF16),
        scratch_shapes=[pltpu.VMEM((tq, tk), F32), pltpu.VMEM((tq, tk), F32),
                        pltpu.VMEM((tq, 1), F32), pltpu.VMEM((tq, dv), F32)],
        compiler_params=_params("parallel", "arbitrary"),
        name="mla_flash",
    )(q, k, v)


def _gelu(x):
    return jax.nn.gelu(x, approximate=True)


def _sg_kernel(u_ref, v_ref, g_ref, w_ref, b_ref, o_ref):
    v = _gelu(v_ref[...].astype(F32))
    mu = jnp.mean(v, axis=-1, keepdims=True)
    vc = v - mu
    vn = (vc * lax.rsqrt(jnp.mean(vc * vc, axis=-1, keepdims=True) + EPS) * g_ref[...]).astype(BF16)
    w = w_ref[...]
    b = b_ref[...]
    for c in range(u_ref.shape[0] // CHUNK):
        rows = slice(c * CHUNK, (c + 1) * CHUNK)
        sp = jnp.dot(w, vn[rows], preferred_element_type=F32) + b
        o_ref[rows, :] = (_gelu(u_ref[rows, :].astype(F32)) * sp).astype(o_ref.dtype)


def spatial_gating(zu, zv, ln_gain, w_spatial, b_spatial):
    s = zu.shape[0]
    tm = _tile(s, 1024)
    blk = pl.BlockSpec((tm, SG_GROUP_DIM), lambda i, g: (i, g))
    return pl.pallas_call(
        _sg_kernel,
        grid=(s // tm, SG_GROUPS),
        in_specs=[blk, blk,
                  pl.BlockSpec((1, SG_GROUP_DIM), lambda i, g: (0, g)),
                  pl.BlockSpec((None, CHUNK, CHUNK), lambda i, g: (g, 0, 0)),
                  pl.BlockSpec((None, CHUNK, 1), lambda i, g: (g, 0, 0))],
        out_specs=blk,
        out_shape=jax.ShapeDtypeStruct((s, SG_WIDTH), BF16),
        compiler_params=_params("parallel", "parallel"),
        name="spatial_gating",
    )(zu, zv, _row(ln_gain), w_spatial.astype(BF16), b_spatial.astype(F32)[:, :, None])


def _merge_kernel(h_ref, f_ref, a_ref, s_ref, wg0_ref, wg1_ref, wg2_ref, b0_ref, b1_ref, b2_ref,
                  wf_ref, wa_ref, ws_ref, o_ref):
    h = h_ref[...]

    def gated(wg_ref, b_ref, y_ref, w_ref):
        gate = jax.nn.sigmoid(jnp.dot(h, wg_ref[...], preferred_element_type=F32) + b_ref[...])
        return gate * jnp.dot(y_ref[...], w_ref[...], preferred_element_type=F32)

    o_ref[...] = (gated(wg0_ref, b0_ref, f_ref, wf_ref) + gated(wg1_ref, b1_ref, a_ref, wa_ref)
                  + gated(wg2_ref, b2_ref, s_ref, ws_ref)).astype(o_ref.dtype)


def gated_merge(h, yf, ya, ys, w_gate, b_gate, w_f, w_a, w_s):
    s, d = h.shape
    tm = _tile(s, 512)
    tn = _tile(d, 256)
    nj = d // tn
    bg = _row(b_gate)

    def rows(width):
        return pl.BlockSpec((tm, width), lambda i, j: (i, 0))

    def gcol(b):
        return pl.BlockSpec((d, tn), lambda i, j, b=b: (0, j + b * nj))

    def bcol(b):
        return pl.BlockSpec((1, tn), lambda i, j, b=b: (0, j + b * nj))

    def wcol(width):
        return pl.BlockSpec((width, tn), lambda i, j: (0, j))

    return pl.pallas_call(
        _merge_kernel,
        grid=(s // tm, nj),
        in_specs=[rows(d), rows(F_WIDTH), rows(MLA_HEADS * V_DIM), rows(SG_WIDTH),
                  gcol(0), gcol(1), gcol(2), bcol(0), bcol(1), bcol(2),
                  wcol(F_WIDTH), wcol(MLA_HEADS * V_DIM), wcol(SG_WIDTH)],
        out_specs=pl.BlockSpec((tm, tn), lambda i, j: (i, j)),
        out_shape=jax.ShapeDtypeStruct((s, d), BF16),
        compiler_params=_params("parallel", "parallel"),
        name="gated_merge",
    )(h, yf, ya, ys, w_gate, w_gate, w_gate, bg, bg, bg, w_f, w_a, w_s)


def _cross_attn_kernel(q_ref, k_ref, v_ref, o_ref, *, scale):
    for hd in range(MEM_HEADS):
        sl = slice(hd * MEM_HEAD_DIM, (hd + 1) * MEM_HEAD_DIM)
        s = lax.dot_general(q_ref[:, sl], k_ref[:, sl], (((1,), (1,)), ((), ())),
                            preferred_element_type=F32) * scale
        p = jnp.exp(s - jnp.max(s, axis=-1, keepdims=True))
        o = jnp.dot(p.astype(BF16), v_ref[:, sl], preferred_element_type=F32)
        o_ref[:, sl] = (o / jnp.sum(p, axis=-1, keepdims=True)).astype(o_ref.dtype)


def cross_attention(q, k, v):
    s, width = q.shape
    m = k.shape[0]
    tm = _tile(s, 1024)
    return pl.pallas_call(
        functools.partial(_cross_attn_kernel, scale=1.0 / math.sqrt(MEM_HEAD_DIM)),
        grid=(s // tm,),
        in_specs=[pl.BlockSpec((tm, width), lambda i: (i, 0)),
                  pl.BlockSpec((m, width), lambda i: (0, 0)),
                  pl.BlockSpec((m, width), lambda i: (0, 0))],
        out_specs=pl.BlockSpec((tm, width), lambda i: (i, 0)),
        out_shape=jax.ShapeDtypeStruct((s, width), BF16),
        compiler_params=_params("parallel"),
        name="cross_attention",
    )(q, k, v)


def _ffn_up_kernel(h_ref, hp_ref, hn_ref, wg_ref, wv_ref, cwg_ref, cwv_ref, cbg_ref, cbv_ref, o_ref):
    i = pl.program_id(0)
    tm = h_ref.shape[0]
    h = h_ref[...]
    hp = jnp.where(i > 0, hp_ref[...], jnp.zeros_like(hp_ref))
    hn = jnp.where(i < pl.num_programs(0) - 1, hn_ref[...], jnp.zeros_like(hn_ref))
    row = lax.broadcasted_iota(jnp.int32, (tm, 1), 0)

    def conv(w_ref, cw_ref, cb_ref):
        w = w_ref[...]
        u = jnp.dot(h, w, preferred_element_type=F32)
        up = jnp.dot(hp, w, preferred_element_type=F32)[7:8]
        un = jnp.dot(hn, w, preferred_element_type=F32)[0:1]
        above = jnp.where(row == 0, up, pltpu.roll(u, 1, axis=0))
        below = jnp.where(row == tm - 1, un, pltpu.roll(u, tm - 1, axis=0))
        cw = cw_ref[...]
        return above * cw[0:1] + u * cw[1:2] + below * cw[2:3] + cb_ref[...]

    o_ref[...] = (_gelu(conv(wg_ref, cwg_ref, cbg_ref)) * conv(wv_ref, cwv_ref, cbv_ref)).astype(o_ref.dtype)


def ffn_up(h, w_up, conv_w, conv_b):
    s, d = h.shape
    d_ff = w_up.shape[1] // 2
    tm = _tile(s, 1024)
    tn = _tile(d_ff, 256)
    nj = d_ff // tn
    hb = tm // 8
    last = s // 8 - 1
    cb = _row(conv_b)
    cw = conv_w.astype(F32)
    return pl.pallas_call(
        _ffn_up_kernel,
        grid=(s // tm, nj),
        in_specs=[pl.BlockSpec((tm, d), lambda i, j: (i, 0)),
                  pl.BlockSpec((8, d), lambda i, j: (jnp.maximum(i * hb - 1, 0), 0)),
                  pl.BlockSpec((8, d), lambda i, j: (jnp.minimum((i + 1) * hb, last), 0)),
                  pl.BlockSpec((d, tn), lambda i, j: (0, j)),
                  pl.BlockSpec((d, tn), lambda i, j: (0, j + nj)),
                  pl.BlockSpec((3, tn), lambda i, j: (0, j)),
                  pl.BlockSpec((3, tn), lambda i, j: (0, j + nj)),
                  pl.BlockSpec((1, tn), lambda i, j: (0, j)),
                  pl.BlockSpec((1, tn), lambda i, j: (0, j + nj))],
        out_specs=pl.BlockSpec((tm, tn), lambda i, j: (i, j)),
        out_shape=jax.ShapeDtypeStruct((s, d_ff), BF16),
        compiler_params=_params("parallel", "parallel"),
        name="ffn_up_conv",
    )(h, h, h, w_up, w_up, cw, cw, cb, cb)


def _split_w_in(w_in):
    half = QK_ROPE // 2
    w_kr = w_in[:, OFF_KR:OFF_SG]
    w_kr_rot = jnp.concatenate([-w_kr[:, half:], w_kr[:, :half]], axis=1)
    return (w_in[:, :OFF_CQ], w_in[:, OFF_CQ:OFF_CKV],
            jnp.concatenate([w_in[:, OFF_CKV:OFF_KR], w_kr, w_kr_rot], axis=1),
            w_in[:, OFF_SG:OFF_SG + SG_WIDTH], w_in[:, OFF_SG + SG_WIDTH:])


def _prep_w_uq(w_uq):
    half = QK_ROPE // 2
    w = w_uq.reshape(Q_LORA, MLA_HEADS, QK_NOPE + QK_ROPE)
    rope = w[:, :, QK_NOPE:]
    rot = jnp.concatenate([-rope[:, :, half:], rope[:, :, :half]], axis=2)
    return jnp.concatenate([w, rot], axis=2).transpose(1, 0, 2).astype(BF16)


def _prep_w_ukv(w_ukv):
    return w_ukv.reshape(KV_LORA, MLA_HEADS, QK_NOPE + V_DIM).transpose(1, 0, 2).astype(BF16)


def kernel(x, mem, positions, mix_pre_norm, mix_post_norm, w_in, mla_q_norm, w_uq, mla_kv_norm, w_ukv, sg_norm, w_spatial, b_spatial, w_br_f, w_br_a, w_br_s, w_gate, b_gate, w_out, mem_pre_norm, mem_post_norm, mem_kv_norm, w_mq, w_mk, w_mv, w_mo, ffn_pre_norm, ffn_post_norm, w_up, conv_w, conv_b, w_down):
    batch, s, d = x.shape
    depth = w_in.shape[0]
    d_ff = w_down.shape[1]
    tk_down = d_ff // 2 if (d_ff // 2) % LANES == 0 else None

    inv_freq = ROPE_THETA ** (-jnp.arange(0, QK_ROPE, 2, dtype=F32) / QK_ROPE)
    tables = _fourier_tables(s)
    outs = []
    for b in range(batch):
        ang = positions[b].astype(F32)[:, None] * inv_freq
        cos, sin = jnp.cos(ang), jnp.sin(ang)
        cs = jnp.concatenate([cos, cos, sin, sin], axis=1)
        xb = x[b]
        mem_b = mem[b]
        h = rmsnorm(xb, mix_pre_norm[0])
        for l in range(depth):
            w_f, w_cq, w_ckvr, w_su, w_sv = [w.astype(BF16) for w in _split_w_in(w_in[l])]
            zf = matmul(h, w_f, name="in_fourier")
            cq = matmul(h, w_cq, name="in_cq")
            ckvr = matmul(h, w_ckvr, name="in_ckv_krope")
            zu = matmul(h, w_su, name="in_sg_u")
            zv = matmul(h, w_sv, name="in_sg_v")
            yf = fourier_mix(zf, tables)
            ya = mla_mix(cq, ckvr, cs, mla_q_norm[l], _prep_w_uq(w_uq[l]),
                         mla_kv_norm[l], _prep_w_ukv(w_ukv[l]))
            ys = spatial_gating(zu, zv, sg_norm[l], w_spatial[l], b_spatial[l])
            merged = gated_merge(h, yf, ya, ys, w_gate[l].astype(BF16), b_gate[l],
                                 w_br_f[l].astype(BF16), w_br_a[l].astype(BF16), w_br_s[l].astype(BF16))
            y = matmul(merged, w_out[l].astype(BF16), name="mix_out")
            xb, h = residual_norm(y, xb, mix_post_norm[l], mem_pre_norm[l])

            mem_n = rmsnorm(mem_b, mem_kv_norm[l])
            q = matmul(h, w_mq[l].astype(BF16), name="mem_q")
            k = matmul(mem_n, w_mk[l].astype(BF16), name="mem_k")
            v = matmul(mem_n, w_mv[l].astype(BF16), name="mem_v")
            o = cross_attention(q, k, v)
            y = matmul(o, w_mo[l].astype(BF16), name="mem_out")
            xb, h = residual_norm(y, xb, mem_post_norm[l], ffn_pre_norm[l])

            act = ffn_up(h, w_up[l].astype(BF16), conv_w[l], conv_b[l])
            y = matmul(act, w_down[l].astype(BF16), tn=512, tk=tk_down, name="ffn_down")
            g_next = mix_pre_norm[l + 1] if l + 1 < depth else None
            xb, h = residual_norm(y, xb, ffn_post_norm[l], g_next)
        outs.append(xb)
    return jnp.stack(outs, axis=0)
```

```python
import functools
import math

import jax
import jax.numpy as jnp
from jax import lax
from jax.experimental import pallas as pl
from jax.experimental.pallas import tpu as pltpu

F32 = jnp.float32
BF16 = jnp.bfloat16
EPS = 1e-6

V7X_VMEM_BYTES = 64 * 1024 * 1024
VMEM_LIMIT_BYTES = V7X_VMEM_BYTES - 4 * 1024 * 1024
LANES = 128
FLASH_UNROLL = 4

F_GROUPS = 4
F_GROUP_DIM = 256
F_WIDTH = F_GROUPS * F_GROUP_DIM
MLA_HEADS = 8
QK_NOPE = 128
QK_ROPE = 64
V_DIM = 128
Q_LORA = 768
KV_LORA = 512
ROPE_THETA = 10000.0
SG_GROUPS = 16
SG_GROUP_DIM = 128
SG_WIDTH = SG_GROUPS * SG_GROUP_DIM
CHUNK = 128
MEM_HEADS = 4
MEM_HEAD_DIM = 256
OFF_CQ = F_WIDTH
OFF_CKV = OFF_CQ + Q_LORA
OFF_KR = OFF_CKV + KV_LORA
OFF_SG = OFF_KR + QK_ROPE


def _params(*sem):
    return pltpu.CompilerParams(dimension_semantics=sem, vmem_limit_bytes=VMEM_LIMIT_BYTES)


def _tile(n, pref):
    if n <= pref:
        return n
    t = pref
    while n % t:
        t //= 2
    return t


def _row(v):
    return v.reshape(1, -1).astype(F32)


def _rmsnorm_kernel(x_ref, g_ref, o_ref):
    x = x_ref[...].astype(F32)
    ms = jnp.mean(x * x, axis=-1, keepdims=True)
    o_ref[...] = (x * lax.rsqrt(ms + EPS) * g_ref[...]).astype(o_ref.dtype)


def rmsnorm(x, g, out_dtype=BF16):
    s, d = x.shape
    tm = _tile(s, 512)
    return pl.pallas_call(
        _rmsnorm_kernel,
        grid=(s // tm,),
        in_specs=[pl.BlockSpec((tm, d), lambda i: (i, 0)), pl.BlockSpec((1, d), lambda i: (0, 0))],
        out_specs=pl.BlockSpec((tm, d), lambda i: (i, 0)),
        out_shape=jax.ShapeDtypeStruct((s, d), out_dtype),
        compiler_params=_params("parallel"),
        name="rmsnorm",
    )(x, _row(g))


def _residual_norm_kernel(y_ref, x_ref, gp_ref, gn_ref, xo_ref, ho_ref):
    y = y_ref[...].astype(F32)
    yn = y * lax.rsqrt(jnp.mean(y * y, axis=-1, keepdims=True) + EPS) * gp_ref[...]
    xn = x_ref[...] + yn
    xo_ref[...] = xn
    ho_ref[...] = (xn * lax.rsqrt(jnp.mean(xn * xn, axis=-1, keepdims=True) + EPS)
                   * gn_ref[...]).astype(ho_ref.dtype)


def _residual_only_kernel(y_ref, x_ref, gp_ref, xo_ref):
    y = y_ref[...].astype(F32)
    yn = y * lax.rsqrt(jnp.mean(y * y, axis=-1, keepdims=True) + EPS) * gp_ref[...]
    xo_ref[...] = x_ref[...] + yn


def residual_norm(y, x, g_post, g_next):
    s, d = x.shape
    tm = _tile(s, 256)
    blk = pl.BlockSpec((tm, d), lambda i: (i, 0))
    vec = pl.BlockSpec((1, d), lambda i: (0, 0))
    if g_next is None:
        return pl.pallas_call(
            _residual_only_kernel,
            grid=(s // tm,),
            in_specs=[blk, blk, vec],
            out_specs=blk,
            out_shape=jax.ShapeDtypeStruct((s, d), F32),
            compiler_params=_params("parallel"),
            name="residual_final",
        )(y, x, _row(g_post)), None
    return pl.pallas_call(
        _residual_norm_kernel,
        grid=(s // tm,),
        in_specs=[blk, blk, vec, vec],
        out_specs=[blk, blk],
        out_shape=[jax.ShapeDtypeStruct((s, d), F32), jax.ShapeDtypeStruct((s, d), BF16)],
        compiler_params=_params("parallel"),
        name="residual_norm",
    )(y, x, _row(g_post), _row(g_next))


def _mm_kernel(a_ref, w_ref, o_ref):
    o_ref[...] = jnp.dot(a_ref[...], w_ref[...], preferred_element_type=F32).astype(o_ref.dtype)


def _mm_acc_kernel(a_ref, w_ref, o_ref, acc_ref):
    k = pl.program_id(2)

    @pl.when(k == 0)
    def _():
        acc_ref[...] = jnp.zeros_like(acc_ref)

    acc_ref[...] += jnp.dot(a_ref[...], w_ref[...], preferred_element_type=F32)

    @pl.when(k == pl.num_programs(2) - 1)
    def _():
        o_ref[...] = acc_ref[...].astype(o_ref.dtype)


def matmul(a, w, layer, *, tm=1024, tn=1024, tk=None, out_dtype=BF16, name="matmul"):
    m, kdim = a.shape
    n = w.shape[2]
    tm = _tile(m, tm)
    tn = _tile(n, tn)
    if tk is None or tk >= kdim:
        return pl.pallas_call(
            _mm_kernel,
            grid=(m // tm, n // tn),
            in_specs=[pl.BlockSpec((tm, kdim), lambda i, j: (i, 0)),
                      pl.BlockSpec((None, kdim, tn), lambda i, j: (layer, 0, j))],
            out_specs=pl.BlockSpec((tm, tn), lambda i, j: (i, j)),
            out_shape=jax.ShapeDtypeStruct((m, n), out_dtype),
            compiler_params=_params("parallel", "parallel"),
            name=name,
        )(a, w)
    assert kdim % tk == 0
    return pl.pallas_call(
        _mm_acc_kernel,
        grid=(m // tm, n // tn, kdim // tk),
        in_specs=[pl.BlockSpec((tm, tk), lambda i, j, k: (i, k)),
                  pl.BlockSpec((None, tk, tn), lambda i, j, k: (layer, k, j))],
        out_specs=pl.BlockSpec((tm, tn), lambda i, j, k: (i, j)),
        out_shape=jax.ShapeDtypeStruct((m, n), out_dtype),
        scratch_shapes=[pltpu.VMEM((tm, tn), F32)],
        compiler_params=_params("parallel", "parallel", "arbitrary"),
        name=name,
    )(a, w)


def _dft_split(s):
    a = 1 << ((s.bit_length() - 1 + 1) // 2)
    assert s % a == 0
    return a, s // a


def _fourier_tables(s):
    a, b = _dft_split(s)
    two_pi = 2.0 * math.pi

    def cs(idx, period):
        ang = (idx % period).astype(F32) * (two_pi / period)
        return jnp.cos(ang), jnp.sin(ang)

    k1 = jnp.arange(a, dtype=jnp.int32)
    c1, s1 = cs(k1[:, None] * k1[None, :], a)
    w1 = jnp.concatenate([c1, -s1], axis=0) * (1.0 / math.sqrt(s))
    k2 = jnp.arange(b, dtype=jnp.int32)
    kk = k1[:, None, None] + a * k2[None, :, None]
    c2, s2 = cs(kk * k2[None, None, :], s)
    g = jnp.concatenate([jnp.concatenate([c2, s2], axis=2),
                         jnp.concatenate([-s2, c2], axis=2)], axis=1)
    c = jnp.arange(F_GROUP_DIM, dtype=jnp.int32)
    cc, sc = cs(c[:, None] * c[None, :], F_GROUP_DIM)
    wc = jnp.concatenate([cc, sc], axis=0) * (1.0 / math.sqrt(F_GROUP_DIM))
    return w1.astype(BF16), g.astype(BF16), wc.astype(BF16)


def _dft1_kernel(w_ref, z_ref, o_ref):
    o_ref[...] = jnp.dot(w_ref[...], z_ref[...], preferred_element_type=F32).astype(o_ref.dtype)


def _dft2_kernel(g_ref, t_ref, wc_ref, o_ref, *, nb, bdim):
    xr, xi = [], []
    for j in range(nb):
        t = t_ref[:, j].reshape(2 * bdim, F_WIDTH)
        x = jnp.dot(g_ref[j], t, preferred_element_type=F32)
        xr.append(x[:bdim].astype(BF16))
        xi.append(x[bdim:].astype(BF16))
    xr = jnp.concatenate(xr, axis=0)
    xi = jnp.concatenate(xi, axis=0)
    for grp in range(F_GROUPS):
        sl = slice(grp * F_GROUP_DIM, (grp + 1) * F_GROUP_DIM)
        y = (jnp.dot(xr[:, sl], wc_ref[:F_GROUP_DIM], preferred_element_type=F32)
             + jnp.dot(xi[:, sl], wc_ref[F_GROUP_DIM:], preferred_element_type=F32))
        for j in range(nb):
            o_ref[:, j * F_WIDTH + grp * F_GROUP_DIM: j * F_WIDTH + (grp + 1) * F_GROUP_DIM] = (
                y[j * bdim:(j + 1) * bdim].astype(o_ref.dtype))


def fourier_mix(zf, tables):
    s = zf.shape[0]
    a, b = _dft_split(s)
    w1, g, wc = tables
    cols = b * F_WIDTH
    tn = _tile(cols, 8 * F_WIDTH)
    t = pl.pallas_call(
        _dft1_kernel,
        grid=(cols // tn,),
        in_specs=[pl.BlockSpec((2 * a, a), lambda j: (0, 0)),
                  pl.BlockSpec((a, tn), lambda j: (0, j))],
        out_specs=pl.BlockSpec((2 * a, tn), lambda j: (0, j)),
        out_shape=jax.ShapeDtypeStruct((2 * a, cols), BF16),
        compiler_params=_params("parallel"),
        name="dft_stage1",
    )(w1, zf.reshape(a, cols))
    nb = _tile(a, 8)
    out = pl.pallas_call(
        functools.partial(_dft2_kernel, nb=nb, bdim=b),
        grid=(a // nb,),
        in_specs=[pl.BlockSpec((nb, 2 * b, 2 * b), lambda i: (i, 0, 0)),
                  pl.BlockSpec((2, nb, b, F_WIDTH), lambda i: (0, i, 0, 0)),
                  pl.BlockSpec((2 * F_GROUP_DIM, F_GROUP_DIM), lambda i: (0, 0))],
        out_specs=pl.BlockSpec((b, nb * F_WIDTH), lambda i: (0, i)),
        out_shape=jax.ShapeDtypeStruct((b, a * F_WIDTH), BF16),
        compiler_params=_params("parallel"),
        name="dft_stage2",
    )(g, t.reshape(2, a, b, F_WIDTH), wc)
    return out.reshape(s, F_WIDTH)


def _rope_lanes(y2, cs):
    w = y2 * cs
    return w + pltpu.roll(w, QK_ROPE, axis=1)


def _mla_q_kernel(cq_ref, gq_ref, cs_ref, w_ref, q_ref, n_ref, *, scale):
    @pl.when(pl.program_id(1) == 0)
    def _():
        c = cq_ref[...].astype(F32)
        n_ref[...] = (c * lax.rsqrt(jnp.mean(c * c, axis=-1, keepdims=True) + EPS)
                      * gq_ref[...]).astype(BF16)

    y = jnp.dot(n_ref[...], w_ref[...], preferred_element_type=F32) * scale
    q_ref[:, :QK_NOPE] = y[:, :QK_NOPE].astype(q_ref.dtype)
    q_ref[:, QK_NOPE:] = _rope_lanes(y[:, QK_NOPE:], cs_ref[...]).astype(q_ref.dtype)


def _mla_kv_kernel(ckv_ref, kr_ref, gkv_ref, cs_ref, w_ref, k_ref, v_ref, n_ref, r_ref):
    @pl.when(pl.program_id(1) == 0)
    def _():
        c = ckv_ref[...].astype(F32)
        n_ref[...] = (c * lax.rsqrt(jnp.mean(c * c, axis=-1, keepdims=True) + EPS)
                      * gkv_ref[...]).astype(BF16)
        r = _rope_lanes(kr_ref[...].astype(F32), cs_ref[...])
        lane = lax.broadcasted_iota(jnp.int32, r.shape, 1)
        r_ref[...] = jnp.where(lane < QK_ROPE, r, 0.0).astype(BF16)

    y = jnp.dot(n_ref[...], w_ref[...], preferred_element_type=F32)
    k_ref[:, :QK_NOPE] = y[:, :QK_NOPE].astype(k_ref.dtype)
    k_ref[:, QK_NOPE:] = r_ref[...]
    v_ref[:, :V_DIM] = y[:, QK_NOPE:].astype(v_ref.dtype)
    v_ref[:, V_DIM:] = jnp.ones((v_ref.shape[0], V_DIM), v_ref.dtype)


def _flash_kernel(q_ref, k_ref, v_ref, o_ref, sa_ref, sb_ref, m_ref, acc_ref, *, tk):
    nk = k_ref.shape[0] // tk
    q = q_ref[...]

    def scores(c):
        off = pl.multiple_of(c * tk, tk)
        return lax.dot_general(q, k_ref[pl.ds(off, tk), :], (((1,), (1,)), ((), ())),
                               preferred_element_type=F32)

    def accumulate(c, s_ref):
        off = pl.multiple_of(c * tk, tk)
        s = s_ref[...]
        m_prev = m_ref[...]
        m_new = jnp.maximum(m_prev, jnp.max(s, axis=-1, keepdims=True))
        p = jnp.exp2(s - m_new).astype(BF16)
        acc_ref[...] = (jnp.exp2(m_prev - m_new) * acc_ref[...]
                        + jnp.dot(p, v_ref[pl.ds(off, tk), :], preferred_element_type=F32))
        m_ref[...] = m_new

    m_ref[...] = jnp.full_like(m_ref, -jnp.inf)
    acc_ref[...] = jnp.zeros_like(acc_ref)
    sa_ref[...] = scores(0)

    bufs = (sa_ref, sb_ref)
    trips = (nk - 1) // FLASH_UNROLL

    def body(t, carry):
        a = FLASH_UNROLL * t
        for u in range(FLASH_UNROLL):
            bufs[(u + 1) % 2][...] = scores(a + u + 1)
            accumulate(a + u, bufs[u % 2])
        return carry

    lax.fori_loop(0, trips, body, 0)
    for c in range(trips * FLASH_UNROLL, nk):
        if c + 1 < nk:
            bufs[(c + 1) % 2][...] = scores(c + 1)
        accumulate(c, bufs[c % 2])
    acc = acc_ref[...]
    o_ref[...] = (acc[:, :V_DIM] / acc[:, V_DIM:]).astype(o_ref.dtype)


def mla_mix(cq, ckvr, cs, g_q, wq, g_kv, wkv):
    s = cq.shape[0]
    tm = _tile(s, 1024)
    scale = math.log2(math.e) / math.sqrt(QK_NOPE + QK_ROPE)
    dq = QK_NOPE + 2 * QK_ROPE
    dv = 2 * V_DIM
    q = pl.pallas_call(
        functools.partial(_mla_q_kernel, scale=scale),
        grid=(s // tm, MLA_HEADS),
        in_specs=[pl.BlockSpec((tm, Q_LORA), lambda i, h: (i, 0)),
                  pl.BlockSpec((1, Q_LORA), lambda i, h: (0, 0)),
                  pl.BlockSpec((tm, LANES), lambda i, h: (i, 0)),
                  pl.BlockSpec((None, Q_LORA, dq), lambda i, h: (h, 0, 0))],
        out_specs=pl.BlockSpec((None, tm, dq), lambda i, h: (h, i, 0)),
        out_shape=jax.ShapeDtypeStruct((MLA_HEADS, s, dq), BF16),
        scratch_shapes=[pltpu.VMEM((tm, Q_LORA), BF16)],
        compiler_params=_params("parallel", "arbitrary"),
        name="mla_q",
    )(cq, _row(g_q), cs, wq)
    k, v = pl.pallas_call(
        _mla_kv_kernel,
        grid=(s // tm, MLA_HEADS),
        in_specs=[pl.BlockSpec((tm, KV_LORA), lambda i, h: (i, 0)),
                  pl.BlockSpec((tm, LANES), lambda i, h: (i, KV_LORA // LANES)),
                  pl.BlockSpec((1, KV_LORA), lambda i, h: (0, 0)),
                  pl.BlockSpec((tm, LANES), lambda i, h: (i, 0)),
                  pl.BlockSpec((None, KV_LORA, QK_NOPE + V_DIM), lambda i, h: (h, 0, 0))],
        out_specs=[pl.BlockSpec((None, tm, dq), lambda i, h: (h, i, 0)),
                   pl.BlockSpec((None, tm, dv), lambda i, h: (h, i, 0))],
        out_shape=[jax.ShapeDtypeStruct((MLA_HEADS, s, dq), BF16),
                   jax.ShapeDtypeStruct((MLA_HEADS, s, dv), BF16)],
        scratch_shapes=[pltpu.VMEM((tm, KV_LORA), BF16), pltpu.VMEM((tm, LANES), BF16)],
        compiler_params=_params("parallel", "arbitrary"),
        name="mla_kv",
    )(ckvr, ckvr, _row(g_kv), cs, wkv)
    tq = _tile(s, 1024)
    tk = _tile(s, 1024)
    once = pl.Buffered(1)
    return pl.pallas_call(
        functools.partial(_flash_kernel, tk=tk),
        grid=(MLA_HEADS, s // tq),
        in_specs=[pl.BlockSpec((None, tq, dq), lambda h, i: (h, i, 0)),
                  pl.BlockSpec((None, s, dq), lambda h, i: (h, 0, 0), pipeline_mode=once),
                  pl.BlockSpec((None, s, dv), lambda h, i: (h, 0, 0), pipeline_mode=once)],
        out_specs=pl.BlockSpec((tq, V_DIM), lambda h, i: (i, h)),
        out_shape=jax.ShapeDtypeStruct((s, MLA_HEADS * V_DIM), BF16),
        scratch_shapes=[pltpu.VMEM((tq, tk), F32), pltpu.VMEM((tq, tk), F32),
                        pltpu.VMEM((tq, 1), F32), pltpu.VMEM((tq, dv), F32)],
        compiler_params=_params("parallel", "arbitrary"),
        name="mla_flash",
    )(q, k, v)


def _gelu(x):
    return jax.nn.gelu(x, approximate=True)


def _sg_kernel(u_ref, v_ref, g_ref, w_ref, b_ref, o_ref):
    v = _gelu(v_ref[...].astype(F32))
    mu = jnp.mean(v, axis=-1, keepdims=True)
    vc = v - mu
    vn = (vc * lax.rsqrt(jnp.mean(vc * vc, axis=-1, keepdims=True) + EPS) * g_ref[...]).astype(BF16)
    w = w_ref[...]
    b = b_ref[...]
    for c in range(u_ref.shape[0] // CHUNK):
        rows = slice(c * CHUNK, (c + 1) * CHUNK)
        sp = jnp.dot(w, vn[rows], preferred_element_type=F32) + b
        o_ref[rows, :] = (_gelu(u_ref[rows, :].astype(F32)) * sp).astype(o_ref.dtype)


def spatial_gating(zu, zv, ln_gain, w_spatial, b_spatial):
    s = zu.shape[0]
    tm = _tile(s, 1024)
    blk = pl.BlockSpec((tm, SG_GROUP_DIM), lambda i, g: (i, g))
    return pl.pallas_call(
        _sg_kernel,
        grid=(s // tm, SG_GROUPS),
        in_specs=[blk, blk,
                  pl.BlockSpec((1, SG_GROUP_DIM), lambda i, g: (0, g)),
                  pl.BlockSpec((None, CHUNK, CHUNK), lambda i, g: (g, 0, 0)),
                  pl.BlockSpec((None, CHUNK, 1), lambda i, g: (g, 0, 0))],
        out_specs=blk,
        out_shape=jax.ShapeDtypeStruct((s, SG_WIDTH), BF16),
        compiler_params=_params("parallel", "parallel"),
        name="spatial_gating",
    )(zu, zv, _row(ln_gain), w_spatial.astype(BF16), b_spatial.astype(F32)[:, :, None])


def _merge_kernel(h_ref, f_ref, a_ref, s_ref, wg0_ref, wg1_ref, wg2_ref, b0_ref, b1_ref, b2_ref,
                  wf_ref, wa_ref, ws_ref, o_ref):
    h = h_ref[...]

    def gated(wg_ref, b_ref, y_ref, w_ref):
        gate = jax.nn.sigmoid(jnp.dot(h, wg_ref[...], preferred_element_type=F32) + b_ref[...])
        return gate * jnp.dot(y_ref[...], w_ref[...], preferred_element_type=F32)

    o_ref[...] = (gated(wg0_ref, b0_ref, f_ref, wf_ref) + gated(wg1_ref, b1_ref, a_ref, wa_ref)
                  + gated(wg2_ref, b2_ref, s_ref, ws_ref)).astype(o_ref.dtype)


def gated_merge(h, yf, ya, ys, w_gate, b_gate, w_f, w_a, w_s, layer):
    s, d = h.shape
    tm = _tile(s, 512)
    tn = _tile(d, 256)
    nj = d // tn
    bg = _row(b_gate)

    def rows(width):
        return pl.BlockSpec((tm, width), lambda i, j: (i, 0))

    def gcol(b):
        return pl.BlockSpec((None, d, tn), lambda i, j, b=b: (layer, 0, j + b * nj))

    def bcol(b):
        return pl.BlockSpec((1, tn), lambda i, j, b=b: (0, j + b * nj))

    def wcol(width):
        return pl.BlockSpec((None, width, tn), lambda i, j: (layer, 0, j))

    return pl.pallas_call(
        _merge_kernel,
        grid=(s // tm, nj),
        in_specs=[rows(d), rows(F_WIDTH), rows(MLA_HEADS * V_DIM), rows(SG_WIDTH),
                  gcol(0), gcol(1), gcol(2), bcol(0), bcol(1), bcol(2),
                  wcol(F_WIDTH), wcol(MLA_HEADS * V_DIM), wcol(SG_WIDTH)],
        out_specs=pl.BlockSpec((tm, tn), lambda i, j: (i, j)),
        out_shape=jax.ShapeDtypeStruct((s, d), BF16),
        compiler_params=_params("parallel", "parallel"),
        name="gated_merge",
    )(h, yf, ya, ys, w_gate, w_gate, w_gate, bg, bg, bg, w_f, w_a, w_s)


def _cross_attn_kernel(q_ref, k_ref, v_ref, o_ref, *, scale):
    for hd in range(MEM_HEADS):
        sl = slice(hd * MEM_HEAD_DIM, (hd + 1) * MEM_HEAD_DIM)
        s = lax.dot_general(q_ref[:, sl], k_ref[:, sl], (((1,), (1,)), ((), ())),
                            preferred_element_type=F32) * scale
        p = jnp.exp(s - jnp.max(s, axis=-1, keepdims=True))
        o = jnp.dot(p.astype(BF16), v_ref[:, sl], preferred_element_type=F32)
        o_ref[:, sl] = (o / jnp.sum(p, axis=-1, keepdims=True)).astype(o_ref.dtype)


def cross_attention(q, k, v):
    s, width = q.shape
    m = k.shape[0]
    tm = _tile(s, 1024)
    return pl.pallas_call(
        functools.partial(_cross_attn_kernel, scale=1.0 / math.sqrt(MEM_HEAD_DIM)),
        grid=(s // tm,),
        in_specs=[pl.BlockSpec((tm, width), lambda i: (i, 0)),
                  pl.BlockSpec((m, width), lambda i: (0, 0)),
                  pl.BlockSpec((m, width), lambda i: (0, 0))],
        out_specs=pl.BlockSpec((tm, width), lambda i: (i, 0)),
        out_shape=jax.ShapeDtypeStruct((s, width), BF16),
        compiler_params=_params("parallel"),
        name="cross_attention",
    )(q, k, v)


def _ffn_up_kernel(h_ref, hp_ref, hn_ref, wg_ref, wv_ref, cwg_ref, cwv_ref, cbg_ref, cbv_ref, o_ref):
    i = pl.program_id(0)
    tm = h_ref.shape[0]
    h = h_ref[...]
    hp = jnp.where(i > 0, hp_ref[...], jnp.zeros_like(hp_ref))
    hn = jnp.where(i < pl.num_programs(0) - 1, hn_ref[...], jnp.zeros_like(hn_ref))
    row = lax.broadcasted_iota(jnp.int32, (tm, 1), 0)

    def conv(w_ref, cw_ref, cb_ref):
        w = w_ref[...]
        u = jnp.dot(h, w, preferred_element_type=F32)
        up = jnp.dot(hp, w, preferred_element_type=F32)[7:8]
        un = jnp.dot(hn, w, preferred_element_type=F32)[0:1]
        above = jnp.where(row == 0, up, pltpu.roll(u, 1, axis=0))
        below = jnp.where(row == tm - 1, un, pltpu.roll(u, tm - 1, axis=0))
        cw = cw_ref[...]
        return above * cw[0:1] + u * cw[1:2] + below * cw[2:3] + cb_ref[...]

    o_ref[...] = (_gelu(conv(wg_ref, cwg_ref, cbg_ref)) * conv(wv_ref, cwv_ref, cbv_ref)).astype(o_ref.dtype)


def ffn_up(h, w_up, conv_w, conv_b, layer):
    s, d = h.shape
    d_ff = w_up.shape[2] // 2
    tm = _tile(s, 1024)
    tn = _tile(d_ff, 256)
    nj = d_ff // tn
    hb = tm // 8
    last = s // 8 - 1
    cb = _row(conv_b)
    cw = conv_w.astype(F32)
    return pl.pallas_call(
        _ffn_up_kernel,
        grid=(s // tm, nj),
        in_specs=[pl.BlockSpec((tm, d), lambda i, j: (i, 0)),
                  pl.BlockSpec((8, d), lambda i, j: (jnp.maximum(i * hb - 1, 0), 0)),
                  pl.BlockSpec((8, d), lambda i, j: (jnp.minimum((i + 1) * hb, last), 0)),
                  pl.BlockSpec((None, d, tn), lambda i, j: (layer, 0, j)),
                  pl.BlockSpec((None, d, tn), lambda i, j: (layer, 0, j + nj)),
                  pl.BlockSpec((3, tn), lambda i, j: (0, j)),
                  pl.BlockSpec((3, tn), lambda i, j: (0, j + nj)),
                  pl.BlockSpec((1, tn), lambda i, j: (0, j)),
                  pl.BlockSpec((1, tn), lambda i, j: (0, j + nj))],
        out_specs=pl.BlockSpec((tm, tn), lambda i, j: (i, j)),
        out_shape=jax.ShapeDtypeStruct((s, d_ff), BF16),
        compiler_params=_params("parallel", "parallel"),
        name="ffn_up_conv",
    )(h, h, h, w_up, w_up, cw, cw, cb, cb)


def _split_w_in(w_in):
    half = QK_ROPE // 2
    w_kr = w_in[..., OFF_KR:OFF_SG]
    w_kr_rot = jnp.concatenate([-w_kr[..., half:], w_kr[..., :half]], axis=-1)
    return (w_in[..., :OFF_CQ], w_in[..., OFF_CQ:OFF_CKV],
            jnp.concatenate([w_in[..., OFF_CKV:OFF_KR], w_kr, w_kr_rot], axis=-1),
            w_in[..., OFF_SG:OFF_SG + SG_WIDTH], w_in[..., OFF_SG + SG_WIDTH:])


def _prep_w_uq(w_uq):
    half = QK_ROPE // 2
    w = w_uq.reshape(Q_LORA, MLA_HEADS, QK_NOPE + QK_ROPE)
    rope = w[:, :, QK_NOPE:]
    rot = jnp.concatenate([-rope[:, :, half:], rope[:, :, :half]], axis=2)
    return jnp.concatenate([w, rot], axis=2).transpose(1, 0, 2).astype(BF16)


def _prep_w_ukv(w_ukv):
    return w_ukv.reshape(KV_LORA, MLA_HEADS, QK_NOPE + V_DIM).transpose(1, 0, 2).astype(BF16)


def kernel(x, mem, positions, mix_pre_norm, mix_post_norm, w_in, mla_q_norm, w_uq, mla_kv_norm, w_ukv, sg_norm, w_spatial, b_spatial, w_br_f, w_br_a, w_br_s, w_gate, b_gate, w_out, mem_pre_norm, mem_post_norm, mem_kv_norm, w_mq, w_mk, w_mv, w_mo, ffn_pre_norm, ffn_post_norm, w_up, conv_w, conv_b, w_down):
    batch, s, d = x.shape
    depth = w_in.shape[0]
    d_ff = w_down.shape[1]
    tk_down = d_ff // 2 if (d_ff // 2) % LANES == 0 else None

    w_f, w_cq, w_ckvr, w_su, w_sv = [w.astype(BF16) for w in _split_w_in(w_in)]
    (w_gate, w_br_f, w_br_a, w_br_s, w_out, w_mq, w_mk, w_mv, w_mo, w_up, w_down) = [
        w.astype(BF16) for w in (w_gate, w_br_f, w_br_a, w_br_s, w_out, w_mq, w_mk, w_mv, w_mo,
                                 w_up, w_down)]

    inv_freq = ROPE_THETA ** (-jnp.arange(0, QK_ROPE, 2, dtype=F32) / QK_ROPE)
    tables = _fourier_tables(s)
    outs = []
    for b in range(batch):
        ang = positions[b].astype(F32)[:, None] * inv_freq
        cos, sin = jnp.cos(ang), jnp.sin(ang)
        cs = jnp.concatenate([cos, cos, sin, sin], axis=1)
        xb = x[b]
        mem_b = mem[b]
        h = rmsnorm(xb, mix_pre_norm[0])
        for l in range(depth):
            zf = matmul(h, w_f, l, name="in_fourier")
            cq = matmul(h, w_cq, l, name="in_cq")
            ckvr = matmul(h, w_ckvr, l, name="in_ckv_krope")
            zu = matmul(h, w_su, l, name="in_sg_u")
            zv = matmul(h, w_sv, l, name="in_sg_v")
            yf = fourier_mix(zf, tables)
            ya = mla_mix(cq, ckvr, cs, mla_q_norm[l], _prep_w_uq(w_uq[l]),
                         mla_kv_norm[l], _prep_w_ukv(w_ukv[l]))
            ys = spatial_gating(zu, zv, sg_norm[l], w_spatial[l], b_spatial[l])
            merged = gated_merge(h, yf, ya, ys, w_gate, b_gate[l], w_br_f, w_br_a, w_br_s, l)
            y = matmul(merged, w_out, l, name="mix_out")
            xb, h = residual_norm(y, xb, mix_post_norm[l], mem_pre_norm[l])

            mem_n = rmsnorm(mem_b, mem_kv_norm[l])
            q = matmul(h, w_mq, l, name="mem_q")
            k = matmul(mem_n, w_mk, l, name="mem_k")
            v = matmul(mem_n, w_mv, l, name="mem_v")
            o = cross_attention(q, k, v)
            y = matmul(o, w_mo, l, name="mem_out")
            xb, h = residual_norm(y, xb, mem_post_norm[l], ffn_pre_norm[l])

            act = ffn_up(h, w_up, conv_w[l], conv_b[l], l)
            y = matmul(act, w_down, l, tn=512, tk=tk_down, name="ffn_down")
            g_next = mix_pre_norm[l + 1] if l + 1 < depth else None
            xb, h = residual_norm(y, xb, ffn_post_norm[l], g_next)
        outs.append(xb)
    return jnp.stack(outs, axis=0)
```

```python
import functools
import math

import jax
import jax.numpy as jnp
from jax import lax
from jax.experimental import pallas as pl
from jax.experimental.pallas import tpu as pltpu

F32 = jnp.float32
BF16 = jnp.bfloat16
EPS = 1e-6

V7X_VMEM_BYTES = 64 * 1024 * 1024
VMEM_LIMIT_BYTES = V7X_VMEM_BYTES - 4 * 1024 * 1024
LANES = 128
FLASH_UNROLL = 4

F_GROUPS = 4
F_GROUP_DIM = 256
F_WIDTH = F_GROUPS * F_GROUP_DIM
MLA_HEADS = 8
QK_NOPE = 128
QK_ROPE = 64
V_DIM = 128
Q_LORA = 768
KV_LORA = 512
ROPE_THETA = 10000.0
SG_GROUPS = 16
SG_GROUP_DIM = 128
SG_WIDTH = SG_GROUPS * SG_GROUP_DIM
CHUNK = 128
MEM_HEADS = 4
MEM_HEAD_DIM = 256
OFF_CQ = F_WIDTH
OFF_CKV = OFF_CQ + Q_LORA
OFF_KR = OFF_CKV + KV_LORA
OFF_SG = OFF_KR + QK_ROPE


def _params(*sem):
    return pltpu.CompilerParams(dimension_semantics=sem, vmem_limit_bytes=VMEM_LIMIT_BYTES)


def _tile(n, pref):
    if n <= pref:
        return n
    t = pref
    while n % t:
        t //= 2
    return t


def _row(v):
    return v.reshape(1, -1).astype(F32)


def _rmsnorm_kernel(x_ref, g_ref, o_ref):
    x = x_ref[...].astype(F32)
    ms = jnp.mean(x * x, axis=-1, keepdims=True)
    o_ref[...] = (x * lax.rsqrt(ms + EPS) * g_ref[...]).astype(o_ref.dtype)


def rmsnorm(x, g, out_dtype=BF16):
    s, d = x.shape
    tm = _tile(s, 256)
    return pl.pallas_call(
        _rmsnorm_kernel,
        grid=(s // tm,),
        in_specs=[pl.BlockSpec((tm, d), lambda i: (i, 0)), pl.BlockSpec((1, d), lambda i: (0, 0))],
        out_specs=pl.BlockSpec((tm, d), lambda i: (i, 0)),
        out_shape=jax.ShapeDtypeStruct((s, d), out_dtype),
        compiler_params=_params("parallel"),
        name="rmsnorm",
    )(x, _row(g))


def _residual_norm_kernel(y_ref, x_ref, gp_ref, gn_ref, xo_ref, ho_ref):
    y = y_ref[...].astype(F32)
    yn = y * lax.rsqrt(jnp.mean(y * y, axis=-1, keepdims=True) + EPS) * gp_ref[...]
    xn = x_ref[...] + yn
    xo_ref[...] = xn
    ho_ref[...] = (xn * lax.rsqrt(jnp.mean(xn * xn, axis=-1, keepdims=True) + EPS)
                   * gn_ref[...]).astype(ho_ref.dtype)


def _residual_only_kernel(y_ref, x_ref, gp_ref, xo_ref):
    y = y_ref[...].astype(F32)
    yn = y * lax.rsqrt(jnp.mean(y * y, axis=-1, keepdims=True) + EPS) * gp_ref[...]
    xo_ref[...] = x_ref[...] + yn


def residual_norm(y, x, g_post, g_next):
    s, d = x.shape
    tm = _tile(s, 256)
    blk = pl.BlockSpec((tm, d), lambda i: (i, 0))
    vec = pl.BlockSpec((1, d), lambda i: (0, 0))
    if g_next is None:
        return pl.pallas_call(
            _residual_only_kernel,
            grid=(s // tm,),
            in_specs=[blk, blk, vec],
            out_specs=blk,
            out_shape=jax.ShapeDtypeStruct((s, d), F32),
            compiler_params=_params("parallel"),
            name="residual_final",
        )(y, x, _row(g_post)), None
    return pl.pallas_call(
        _residual_norm_kernel,
        grid=(s // tm,),
        in_specs=[blk, blk, vec, vec],
        out_specs=[blk, blk],
        out_shape=[jax.ShapeDtypeStruct((s, d), F32), jax.ShapeDtypeStruct((s, d), BF16)],
        compiler_params=_params("parallel"),
        name="residual_norm",
    )(y, x, _row(g_post), _row(g_next))


def _mm_kernel(a_ref, w_ref, o_ref):
    o_ref[...] = jnp.dot(a_ref[...], w_ref[...], preferred_element_type=F32).astype(o_ref.dtype)


def _mm_acc_kernel(a_ref, w_ref, o_ref, acc_ref):
    k = pl.program_id(2)

    @pl.when(k == 0)
    def _():
        acc_ref[...] = jnp.zeros_like(acc_ref)

    acc_ref[...] += jnp.dot(a_ref[...], w_ref[...], preferred_element_type=F32)

    @pl.when(k == pl.num_programs(2) - 1)
    def _():
        o_ref[...] = acc_ref[...].astype(o_ref.dtype)


def matmul(a, w, layer, *, tm=1024, tn=1024, tk=None, out_dtype=BF16, name="matmul"):
    m, kdim = a.shape
    n = w.shape[2]
    tm = _tile(m, tm)
    tn = _tile(n, tn)
    if tk is None or tk >= kdim:
        return pl.pallas_call(
            _mm_kernel,
            grid=(m // tm, n // tn),
            in_specs=[pl.BlockSpec((tm, kdim), lambda i, j: (i, 0)),
                      pl.BlockSpec((None, kdim, tn), lambda i, j: (layer, 0, j))],
            out_specs=pl.BlockSpec((tm, tn), lambda i, j: (i, j)),
            out_shape=jax.ShapeDtypeStruct((m, n), out_dtype),
            compiler_params=_params("parallel", "parallel"),
            name=name,
        )(a, w)
    assert kdim % tk == 0
    return pl.pallas_call(
        _mm_acc_kernel,
        grid=(m // tm, n // tn, kdim // tk),
        in_specs=[pl.BlockSpec((tm, tk), lambda i, j, k: (i, k)),
                  pl.BlockSpec((None, tk, tn), lambda i, j, k: (layer, k, j))],
        out_specs=pl.BlockSpec((tm, tn), lambda i, j, k: (i, j)),
        out_shape=jax.ShapeDtypeStruct((m, n), out_dtype),
        scratch_shapes=[pltpu.VMEM((tm, tn), F32)],
        compiler_params=_params("parallel", "parallel", "arbitrary"),
        name=name,
    )(a, w)


def _dft_split(s):
    a = 1 << ((s.bit_length() - 1 + 1) // 2)
    assert s % a == 0
    return a, s // a


def _fourier_tables(s):
    a, b = _dft_split(s)
    two_pi = 2.0 * math.pi

    def cs(idx, period):
        ang = (idx % period).astype(F32) * (two_pi / period)
        return jnp.cos(ang), jnp.sin(ang)

    k1 = jnp.arange(a, dtype=jnp.int32)
    c1, s1 = cs(k1[:, None] * k1[None, :], a)
    w1 = jnp.concatenate([c1, -s1], axis=0) * (1.0 / math.sqrt(s))
    k2 = jnp.arange(b, dtype=jnp.int32)
    kk = k1[:, None, None] + a * k2[None, :, None]
    c2, s2 = cs(kk * k2[None, None, :], s)
    g = jnp.concatenate([jnp.concatenate([c2, s2], axis=2),
                         jnp.concatenate([-s2, c2], axis=2)], axis=1)
    c = jnp.arange(F_GROUP_DIM, dtype=jnp.int32)
    cc, sc = cs(c[:, None] * c[None, :], F_GROUP_DIM)
    wc = jnp.concatenate([cc, sc], axis=0) * (1.0 / math.sqrt(F_GROUP_DIM))
    return w1.astype(BF16), g.astype(BF16), wc.astype(BF16)


def _dft1_kernel(w_ref, z_ref, o_ref):
    o_ref[...] = jnp.dot(w_ref[...], z_ref[...], preferred_element_type=F32).astype(o_ref.dtype)


def _dft2_kernel(g_ref, t_ref, wc_ref, o_ref, *, nb, bdim):
    xr, xi = [], []
    for j in range(nb):
        t = t_ref[:, j].reshape(2 * bdim, F_WIDTH)
        x = jnp.dot(g_ref[j], t, preferred_element_type=F32)
        xr.append(x[:bdim].astype(BF16))
        xi.append(x[bdim:].astype(BF16))
    xr = jnp.concatenate(xr, axis=0)
    xi = jnp.concatenate(xi, axis=0)
    for grp in range(F_GROUPS):
        sl = slice(grp * F_GROUP_DIM, (grp + 1) * F_GROUP_DIM)
        y = (jnp.dot(xr[:, sl], wc_ref[:F_GROUP_DIM], preferred_element_type=F32)
             + jnp.dot(xi[:, sl], wc_ref[F_GROUP_DIM:], preferred_element_type=F32))
        for j in range(nb):
            o_ref[:, j * F_WIDTH + grp * F_GROUP_DIM: j * F_WIDTH + (grp + 1) * F_GROUP_DIM] = (
                y[j * bdim:(j + 1) * bdim].astype(o_ref.dtype))


def fourier_mix(zf, tables):
    s = zf.shape[0]
    a, b = _dft_split(s)
    w1, g, wc = tables
    cols = b * F_WIDTH
    tn = _tile(cols, 8 * F_WIDTH)
    t = pl.pallas_call(
        _dft1_kernel,
        grid=(cols // tn,),
        in_specs=[pl.BlockSpec((2 * a, a), lambda j: (0, 0)),
                  pl.BlockSpec((a, tn), lambda j: (0, j))],
        out_specs=pl.BlockSpec((2 * a, tn), lambda j: (0, j)),
        out_shape=jax.ShapeDtypeStruct((2 * a, cols), BF16),
        compiler_params=_params("parallel"),
        name="dft_stage1",
    )(w1, zf.reshape(a, cols))
    nb = _tile(a, 8)
    out = pl.pallas_call(
        functools.partial(_dft2_kernel, nb=nb, bdim=b),
        grid=(a // nb,),
        in_specs=[pl.BlockSpec((nb, 2 * b, 2 * b), lambda i: (i, 0, 0)),
                  pl.BlockSpec((2, nb, b, F_WIDTH), lambda i: (0, i, 0, 0)),
                  pl.BlockSpec((2 * F_GROUP_DIM, F_GROUP_DIM), lambda i: (0, 0))],
        out_specs=pl.BlockSpec((b, nb * F_WIDTH), lambda i: (0, i)),
        out_shape=jax.ShapeDtypeStruct((b, a * F_WIDTH), BF16),
        compiler_params=_params("parallel"),
        name="dft_stage2",
    )(g, t.reshape(2, a, b, F_WIDTH), wc)
    return out.reshape(s, F_WIDTH)


def _rope_lanes(y2, cs):
    w = y2 * cs
    return w + pltpu.roll(w, QK_ROPE, axis=1)


def _mla_q_kernel(cq_ref, gq_ref, cs_ref, w_ref, q_ref, *, scale):
    c = cq_ref[...].astype(F32)
    n = (c * lax.rsqrt(jnp.mean(c * c, axis=-1, keepdims=True) + EPS) * gq_ref[...]).astype(BF16)
    cs = cs_ref[...]
    for hd in range(MLA_HEADS):
        y = jnp.dot(n, w_ref[hd], preferred_element_type=F32) * scale
        q_ref[hd, :, :QK_NOPE] = y[:, :QK_NOPE].astype(q_ref.dtype)
        q_ref[hd, :, QK_NOPE:] = _rope_lanes(y[:, QK_NOPE:], cs).astype(q_ref.dtype)


def _mla_kv_kernel(ckv_ref, kr_ref, gkv_ref, cs_ref, w_ref, k_ref, v_ref):
    c = ckv_ref[...].astype(F32)
    n = (c * lax.rsqrt(jnp.mean(c * c, axis=-1, keepdims=True) + EPS) * gkv_ref[...]).astype(BF16)
    r = _rope_lanes(kr_ref[...].astype(F32), cs_ref[...])
    lane = lax.broadcasted_iota(jnp.int32, r.shape, 1)
    r = jnp.where(lane < QK_ROPE, r, 0.0).astype(k_ref.dtype)
    ones = jnp.ones((v_ref.shape[1], V_DIM), v_ref.dtype)
    for hd in range(MLA_HEADS):
        y = jnp.dot(n, w_ref[hd], preferred_element_type=F32)
        k_ref[hd, :, :QK_NOPE] = y[:, :QK_NOPE].astype(k_ref.dtype)
        k_ref[hd, :, QK_NOPE:] = r
        v_ref[hd, :, :V_DIM] = y[:, QK_NOPE:].astype(v_ref.dtype)
        v_ref[hd, :, V_DIM:] = ones


def _flash_kernel(q_ref, k_ref, v_ref, o_ref, sa_ref, sb_ref, m_ref, acc_ref, *, tk):
    nk = k_ref.shape[0] // tk
    q = q_ref[...]

    def scores(c):
        off = pl.multiple_of(c * tk, tk)
        return lax.dot_general(q, k_ref[pl.ds(off, tk), :], (((1,), (1,)), ((), ())),
                               preferred_element_type=F32)

    def accumulate(c, s_ref):
        off = pl.multiple_of(c * tk, tk)
        s = s_ref[...]
        m_prev = m_ref[...]
        m_new = jnp.maximum(m_prev, jnp.max(s, axis=-1, keepdims=True))
        p = jnp.exp2(s - m_new).astype(BF16)
        acc_ref[...] = (jnp.exp2(m_prev - m_new) * acc_ref[...]
                        + jnp.dot(p, v_ref[pl.ds(off, tk), :], preferred_element_type=F32))
        m_ref[...] = m_new

    m_ref[...] = jnp.full_like(m_ref, -jnp.inf)
    acc_ref[...] = jnp.zeros_like(acc_ref)
    sa_ref[...] = scores(0)

    bufs = (sa_ref, sb_ref)
    trips = (nk - 1) // FLASH_UNROLL

    def body(t, carry):
        a = FLASH_UNROLL * t
        for u in range(FLASH_UNROLL):
            bufs[(u + 1) % 2][...] = scores(a + u + 1)
            accumulate(a + u, bufs[u % 2])
        return carry

    lax.fori_loop(0, trips, body, 0)
    for c in range(trips * FLASH_UNROLL, nk):
        if c + 1 < nk:
            bufs[(c + 1) % 2][...] = scores(c + 1)
        accumulate(c, bufs[c % 2])
    acc = acc_ref[...]
    o_ref[...] = (acc[:, :V_DIM] / acc[:, V_DIM:]).astype(o_ref.dtype)


def mla_mix(cq, ckvr, cs, g_q, wq, g_kv, wkv):
    s = cq.shape[0]
    tm = _tile(s, 1024)
    scale = math.log2(math.e) / math.sqrt(QK_NOPE + QK_ROPE)
    dq = QK_NOPE + 2 * QK_ROPE
    dv = 2 * V_DIM
    q = pl.pallas_call(
        functools.partial(_mla_q_kernel, scale=scale),
        grid=(s // tm,),
        in_specs=[pl.BlockSpec((tm, Q_LORA), lambda i: (i, 0)),
                  pl.BlockSpec((1, Q_LORA), lambda i: (0, 0)),
                  pl.BlockSpec((tm, LANES), lambda i: (i, 0)),
                  pl.BlockSpec((MLA_HEADS, Q_LORA, dq), lambda i: (0, 0, 0))],
        out_specs=pl.BlockSpec((MLA_HEADS, tm, dq), lambda i: (0, i, 0)),
        out_shape=jax.ShapeDtypeStruct((MLA_HEADS, s, dq), BF16),
        compiler_params=_params("parallel"),
        name="mla_q",
    )(cq, _row(g_q), cs, wq)
    k, v = pl.pallas_call(
        _mla_kv_kernel,
        grid=(s // tm,),
        in_specs=[pl.BlockSpec((tm, KV_LORA), lambda i: (i, 0)),
                  pl.BlockSpec((tm, LANES), lambda i: (i, KV_LORA // LANES)),
                  pl.BlockSpec((1, KV_LORA), lambda i: (0, 0)),
                  pl.BlockSpec((tm, LANES), lambda i: (i, 0)),
                  pl.BlockSpec((MLA_HEADS, KV_LORA, QK_NOPE + V_DIM), lambda i: (0, 0, 0))],
        out_specs=[pl.BlockSpec((MLA_HEADS, tm, dq), lambda i: (0, i, 0)),
                   pl.BlockSpec((MLA_HEADS, tm, dv), lambda i: (0, i, 0))],
        out_shape=[jax.ShapeDtypeStruct((MLA_HEADS, s, dq), BF16),
                   jax.ShapeDtypeStruct((MLA_HEADS, s, dv), BF16)],
        compiler_params=_params("parallel"),
        name="mla_kv",
    )(ckvr, ckvr, _row(g_kv), cs, wkv)
    tq = _tile(s, 1024)
    tk = _tile(s, 1024)
    once = pl.Buffered(1)
    return pl.pallas_call(
        functools.partial(_flash_kernel, tk=tk),
        grid=(MLA_HEADS, s // tq),
        in_specs=[pl.BlockSpec((None, tq, dq), lambda h, i: (h, i, 0)),
                  pl.BlockSpec((None, s, dq), lambda h, i: (h, 0, 0), pipeline_mode=once),
                  pl.BlockSpec((None, s, dv), lambda h, i: (h, 0, 0), pipeline_mode=once)],
        out_specs=pl.BlockSpec((tq, V_DIM), lambda h, i: (i, h)),
        out_shape=jax.ShapeDtypeStruct((s, MLA_HEADS * V_DIM), BF16),
        scratch_shapes=[pltpu.VMEM((tq, tk), F32), pltpu.VMEM((tq, tk), F32),
                        pltpu.VMEM((tq, 1), F32), pltpu.VMEM((tq, dv), F32)],
        compiler_params=_params("parallel", "arbitrary"),
        name="mla_flash",
    )(q, k, v)


def _gelu(x):
    return jax.nn.gelu(x, approximate=True)


def _sg_kernel(u_ref, v_ref, g_ref, w_ref, b_ref, o_ref):
    def group(grp, carry):
        cols = pl.ds(pl.multiple_of(grp * SG_GROUP_DIM, SG_GROUP_DIM), SG_GROUP_DIM)
        v = _gelu(v_ref[:, cols].astype(F32))
        mu = jnp.mean(v, axis=-1, keepdims=True)
        vc = v - mu
        vn = (vc * lax.rsqrt(jnp.mean(vc * vc, axis=-1, keepdims=True) + EPS)
              * g_ref[:, cols]).astype(BF16)
        w = w_ref[grp]
        b = b_ref[grp]
        for c in range(u_ref.shape[0] // CHUNK):
            rows = slice(c * CHUNK, (c + 1) * CHUNK)
            sp = jnp.dot(w, vn[rows], preferred_element_type=F32) + b
            o_ref[rows, cols] = (_gelu(u_ref[rows, cols].astype(F32)) * sp).astype(o_ref.dtype)
        return carry

    lax.fori_loop(0, SG_GROUPS, group, 0)


def spatial_gating(zu, zv, ln_gain, w_spatial, b_spatial):
    s = zu.shape[0]
    tm = _tile(s, 1024)
    blk = pl.BlockSpec((tm, SG_WIDTH), lambda i: (i, 0))
    return pl.pallas_call(
        _sg_kernel,
        grid=(s // tm,),
        in_specs=[blk, blk,
                  pl.BlockSpec((1, SG_WIDTH), lambda i: (0, 0)),
                  pl.BlockSpec((SG_GROUPS, CHUNK, CHUNK), lambda i: (0, 0, 0)),
                  pl.BlockSpec((SG_GROUPS, CHUNK, 1), lambda i: (0, 0, 0))],
        out_specs=blk,
        out_shape=jax.ShapeDtypeStruct((s, SG_WIDTH), BF16),
        compiler_params=_params("parallel"),
        name="spatial_gating",
    )(zu, zv, _row(ln_gain), w_spatial.astype(BF16), b_spatial.astype(F32)[:, :, None])


def _merge_kernel(h_ref, f_ref, a_ref, s_ref, wg0_ref, wg1_ref, wg2_ref, b0_ref, b1_ref, b2_ref,
                  wf_ref, wa_ref, ws_ref, o_ref):
    h = h_ref[...]

    def gated(wg_ref, b_ref, y_ref, w_ref):
        gate = jax.nn.sigmoid(jnp.dot(h, wg_ref[...], preferred_element_type=F32) + b_ref[...])
        return gate * jnp.dot(y_ref[...], w_ref[...], preferred_element_type=F32)

    o_ref[...] = (gated(wg0_ref, b0_ref, f_ref, wf_ref) + gated(wg1_ref, b1_ref, a_ref, wa_ref)
                  + gated(wg2_ref, b2_ref, s_ref, ws_ref)).astype(o_ref.dtype)


def gated_merge(h, yf, ya, ys, w_gate, b_gate, w_f, w_a, w_s, layer):
    s, d = h.shape
    tm = _tile(s, 512)
    tn = _tile(d, 256)
    nj = d // tn
    bg = _row(b_gate)

    def rows(width):
        return pl.BlockSpec((tm, width), lambda i, j: (i, 0))

    def gcol(b):
        return pl.BlockSpec((None, d, tn), lambda i, j, b=b: (layer, 0, j + b * nj))

    def bcol(b):
        return pl.BlockSpec((1, tn), lambda i, j, b=b: (0, j + b * nj))

    def wcol(width):
        return pl.BlockSpec((None, width, tn), lambda i, j: (layer, 0, j))

    return pl.pallas_call(
        _merge_kernel,
        grid=(s // tm, nj),
        in_specs=[rows(d), rows(F_WIDTH), rows(MLA_HEADS * V_DIM), rows(SG_WIDTH),
                  gcol(0), gcol(1), gcol(2), bcol(0), bcol(1), bcol(2),
                  wcol(F_WIDTH), wcol(MLA_HEADS * V_DIM), wcol(SG_WIDTH)],
        out_specs=pl.BlockSpec((tm, tn), lambda i, j: (i, j)),
        out_shape=jax.ShapeDtypeStruct((s, d), BF16),
        compiler_params=_params("parallel", "parallel"),
        name="gated_merge",
    )(h, yf, ya, ys, w_gate, w_gate, w_gate, bg, bg, bg, w_f, w_a, w_s)


def _cross_attn_kernel(q_ref, k_ref, v_ref, o_ref, *, scale):
    for hd in range(MEM_HEADS):
        sl = slice(hd * MEM_HEAD_DIM, (hd + 1) * MEM_HEAD_DIM)
        s = lax.dot_general(q_ref[:, sl], k_ref[:, sl], (((1,), (1,)), ((), ())),
                            preferred_element_type=F32) * scale
        p = jnp.exp(s - jnp.max(s, axis=-1, keepdims=True))
        o = jnp.dot(p.astype(BF16), v_ref[:, sl], preferred_element_type=F32)
        o_ref[:, sl] = (o / jnp.sum(p, axis=-1, keepdims=True)).astype(o_ref.dtype)


def cross_attention(q, k, v):
    s, width = q.shape
    m = k.shape[0]
    tm = _tile(s, 1024)
    return pl.pallas_call(
        functools.partial(_cross_attn_kernel, scale=1.0 / math.sqrt(MEM_HEAD_DIM)),
        grid=(s // tm,),
        in_specs=[pl.BlockSpec((tm, width), lambda i: (i, 0)),
                  pl.BlockSpec((m, width), lambda i: (0, 0)),
                  pl.BlockSpec((m, width), lambda i: (0, 0))],
        out_specs=pl.BlockSpec((tm, width), lambda i: (i, 0)),
        out_shape=jax.ShapeDtypeStruct((s, width), BF16),
        compiler_params=_params("parallel"),
        name="cross_attention",
    )(q, k, v)


def _ffn_up_kernel(h_ref, hp_ref, hn_ref, wg_ref, wv_ref, cwg_ref, cwv_ref, cbg_ref, cbv_ref, o_ref):
    i = pl.program_id(0)
    tm = h_ref.shape[0]
    h = h_ref[...]
    hp = jnp.where(i > 0, hp_ref[...], jnp.zeros_like(hp_ref))
    hn = jnp.where(i < pl.num_programs(0) - 1, hn_ref[...], jnp.zeros_like(hn_ref))
    row = lax.broadcasted_iota(jnp.int32, (tm, 1), 0)

    def conv(w_ref, cw_ref, cb_ref):
        w = w_ref[...].astype(BF16)
        u = jnp.dot(h, w, preferred_element_type=F32)
        up = jnp.dot(hp, w, preferred_element_type=F32)[7:8]
        un = jnp.dot(hn, w, preferred_element_type=F32)[0:1]
        above = jnp.where(row == 0, up, pltpu.roll(u, 1, axis=0))
        below = jnp.where(row == tm - 1, un, pltpu.roll(u, tm - 1, axis=0))
        cw = cw_ref[...]
        return above * cw[0:1] + u * cw[1:2] + below * cw[2:3] + cb_ref[...]

    o_ref[...] = (_gelu(conv(wg_ref, cwg_ref, cbg_ref)) * conv(wv_ref, cwv_ref, cbv_ref)).astype(o_ref.dtype)


def ffn_up(h, w_up, conv_w, conv_b, layer):
    s, d = h.shape
    d_ff = w_up.shape[2] // 2
    tm = _tile(s, 1024)
    tn = _tile(d_ff, 256)
    nj = d_ff // tn
    hb = tm // 8
    last = s // 8 - 1
    cb = _row(conv_b)
    cw = conv_w.astype(F32)
    return pl.pallas_call(
        _ffn_up_kernel,
        grid=(s // tm, nj),
        in_specs=[pl.BlockSpec((tm, d), lambda i, j: (i, 0)),
                  pl.BlockSpec((8, d), lambda i, j: (jnp.maximum(i * hb - 1, 0), 0)),
                  pl.BlockSpec((8, d), lambda i, j: (jnp.minimum((i + 1) * hb, last), 0)),
                  pl.BlockSpec((None, d, tn), lambda i, j: (layer, 0, j)),
                  pl.BlockSpec((None, d, tn), lambda i, j: (layer, 0, j + nj)),
                  pl.BlockSpec((3, tn), lambda i, j: (0, j)),
                  pl.BlockSpec((3, tn), lambda i, j: (0, j + nj)),
                  pl.BlockSpec((1, tn), lambda i, j: (0, j)),
                  pl.BlockSpec((1, tn), lambda i, j: (0, j + nj))],
        out_specs=pl.BlockSpec((tm, tn), lambda i, j: (i, j)),
        out_shape=jax.ShapeDtypeStruct((s, d_ff), BF16),
        compiler_params=_params("parallel", "parallel"),
        name="ffn_up_conv",
    )(h, h, h, w_up, w_up, cw, cw, cb, cb)


def _split_w_in(w_in):
    half = QK_ROPE // 2
    w_kr = w_in[..., OFF_KR:OFF_SG]
    w_kr_rot = jnp.concatenate([-w_kr[..., half:], w_kr[..., :half]], axis=-1)
    return (w_in[..., :OFF_CQ], w_in[..., OFF_CQ:OFF_CKV],
            jnp.concatenate([w_in[..., OFF_CKV:OFF_KR], w_kr, w_kr_rot], axis=-1),
            w_in[..., OFF_SG:OFF_SG + SG_WIDTH], w_in[..., OFF_SG + SG_WIDTH:])


def _prep_w_uq(w_uq):
    half = QK_ROPE // 2
    w = w_uq.reshape(Q_LORA, MLA_HEADS, QK_NOPE + QK_ROPE)
    rope = w[:, :, QK_NOPE:]
    rot = jnp.concatenate([-rope[:, :, half:], rope[:, :, :half]], axis=2)
    return jnp.concatenate([w, rot], axis=2).transpose(1, 0, 2).astype(BF16)


def _prep_w_ukv(w_ukv):
    return w_ukv.reshape(KV_LORA, MLA_HEADS, QK_NOPE + V_DIM).transpose(1, 0, 2).astype(BF16)


def kernel(x, mem, positions, mix_pre_norm, mix_post_norm, w_in, mla_q_norm, w_uq, mla_kv_norm, w_ukv, sg_norm, w_spatial, b_spatial, w_br_f, w_br_a, w_br_s, w_gate, b_gate, w_out, mem_pre_norm, mem_post_norm, mem_kv_norm, w_mq, w_mk, w_mv, w_mo, ffn_pre_norm, ffn_post_norm, w_up, conv_w, conv_b, w_down):
    batch, s, d = x.shape
    depth = w_in.shape[0]

    w_f, w_cq, w_ckvr, w_su, w_sv = [w.astype(BF16) for w in _split_w_in(w_in)]
    (w_gate, w_br_f, w_br_a, w_br_s, w_out, w_mq, w_mk, w_mv, w_mo, w_down) = [
        w.astype(BF16) for w in (w_gate, w_br_f, w_br_a, w_br_s, w_out, w_mq, w_mk, w_mv, w_mo,
                                 w_down)]

    inv_freq = ROPE_THETA ** (-jnp.arange(0, QK_ROPE, 2, dtype=F32) / QK_ROPE)
    tables = _fourier_tables(s)
    outs = []
    for b in range(batch):
        ang = positions[b].astype(F32)[:, None] * inv_freq
        cos, sin = jnp.cos(ang), jnp.sin(ang)
        cs = jnp.concatenate([cos, cos, sin, sin], axis=1)
        xb = x[b]
        mem_b = mem[b]
        h = rmsnorm(xb, mix_pre_norm[0])
        for l in range(depth):
            zf = matmul(h, w_f, l, name="in_fourier")
            cq = matmul(h, w_cq, l, name="in_cq")
            ckvr = matmul(h, w_ckvr, l, name="in_ckv_krope")
            zu = matmul(h, w_su, l, name="in_sg_u")
            zv = matmul(h, w_sv, l, name="in_sg_v")
            yf = fourier_mix(zf, tables)
            ya = mla_mix(cq, ckvr, cs, mla_q_norm[l], _prep_w_uq(w_uq[l]),
                         mla_kv_norm[l], _prep_w_ukv(w_ukv[l]))
            ys = spatial_gating(zu, zv, sg_norm[l], w_spatial[l], b_spatial[l])
            merged = gated_merge(h, yf, ya, ys, w_gate, b_gate[l], w_br_f, w_br_a, w_br_s, l)
            y = matmul(merged, w_out, l, name="mix_out")
            xb, h = residual_norm(y, xb, mix_post_norm[l], mem_pre_norm[l])

            mem_n = rmsnorm(mem_b, mem_kv_norm[l])
            q = matmul(h, w_mq, l, name="mem_q")
            k = matmul(mem_n, w_mk, l, name="mem_k")
            v = matmul(mem_n, w_mv, l, name="mem_v")
            o = cross_attention(q, k, v)
            y = matmul(o, w_mo, l, name="mem_out")
            xb, h = residual_norm(y, xb, mem_post_norm[l], ffn_pre_norm[l])

            act = ffn_up(h, w_up, conv_w[l], conv_b[l], l)
            y = matmul(act, w_down, l, tm=512, tn=512, name="ffn_down")
            g_next = mix_pre_norm[l + 1] if l + 1 < depth else None
            xb, h = residual_norm(y, xb, ffn_post_norm[l], g_next)
        outs.append(xb)
    return outs[0][None] if batch == 1 else jnp.stack(outs, axis=0)
```

```python
import functools
import math

import jax
import jax.numpy as jnp
from jax import lax
from jax.experimental import pallas as pl
from jax.experimental.pallas import tpu as pltpu

F32 = jnp.float32
BF16 = jnp.bfloat16
EPS = 1e-6

V7X_VMEM_BYTES = 64 * 1024 * 1024
VMEM_LIMIT_BYTES = V7X_VMEM_BYTES - 4 * 1024 * 1024
LANES = 128
FLASH_UNROLL = 4

F_GROUPS = 4
F_GROUP_DIM = 256
F_WIDTH = F_GROUPS * F_GROUP_DIM
MLA_HEADS = 8
QK_NOPE = 128
QK_ROPE = 64
V_DIM = 128
Q_LORA = 768
KV_LORA = 512
ROPE_THETA = 10000.0
SG_GROUPS = 16
SG_GROUP_DIM = 128
SG_WIDTH = SG_GROUPS * SG_GROUP_DIM
CHUNK = 128
MEM_HEADS = 4
MEM_HEAD_DIM = 256
OFF_CQ = F_WIDTH
OFF_CKV = OFF_CQ + Q_LORA
OFF_KR = OFF_CKV + KV_LORA
OFF_SG = OFF_KR + QK_ROPE


def _params(*sem):
    return pltpu.CompilerParams(dimension_semantics=sem, vmem_limit_bytes=VMEM_LIMIT_BYTES)


def _tile(n, pref):
    if n <= pref:
        return n
    t = pref
    while n % t:
        t //= 2
    return t


def _row(v):
    return v.reshape(1, -1).astype(F32)


def _rmsnorm_kernel(x_ref, g_ref, o_ref):
    x = x_ref[...].astype(F32)
    ms = jnp.mean(x * x, axis=-1, keepdims=True)
    o_ref[...] = (x * lax.rsqrt(ms + EPS) * g_ref[...]).astype(o_ref.dtype)


def rmsnorm(x, g, out_dtype=BF16):
    s, d = x.shape
    tm = _tile(s, 256)
    return pl.pallas_call(
        _rmsnorm_kernel,
        grid=(s // tm,),
        in_specs=[pl.BlockSpec((tm, d), lambda i: (i, 0)), pl.BlockSpec((1, d), lambda i: (0, 0))],
        out_specs=pl.BlockSpec((tm, d), lambda i: (i, 0)),
        out_shape=jax.ShapeDtypeStruct((s, d), out_dtype),
        compiler_params=_params("parallel"),
        name="rmsnorm",
    )(x, _row(g))


def _residual_norm_kernel(y_ref, x_ref, gp_ref, gn_ref, xo_ref, ho_ref):
    y = y_ref[...].astype(F32)
    yn = y * lax.rsqrt(jnp.mean(y * y, axis=-1, keepdims=True) + EPS) * gp_ref[...]
    xn = x_ref[...] + yn
    xo_ref[...] = xn
    ho_ref[...] = (xn * lax.rsqrt(jnp.mean(xn * xn, axis=-1, keepdims=True) + EPS)
                   * gn_ref[...]).astype(ho_ref.dtype)


def _residual_only_kernel(y_ref, x_ref, gp_ref, xo_ref):
    y = y_ref[...].astype(F32)
    yn = y * lax.rsqrt(jnp.mean(y * y, axis=-1, keepdims=True) + EPS) * gp_ref[...]
    xo_ref[...] = x_ref[...] + yn


def residual_norm(y, x, g_post, g_next):
    s, d = x.shape
    tm = _tile(s, 256)
    blk = pl.BlockSpec((tm, d), lambda i: (i, 0))
    vec = pl.BlockSpec((1, d), lambda i: (0, 0))
    if g_next is None:
        return pl.pallas_call(
            _residual_only_kernel,
            grid=(s // tm,),
            in_specs=[blk, blk, vec],
            out_specs=blk,
            out_shape=jax.ShapeDtypeStruct((s, d), F32),
            compiler_params=_params("parallel"),
            name="residual_final",
        )(y, x, _row(g_post)), None
    return pl.pallas_call(
        _residual_norm_kernel,
        grid=(s // tm,),
        in_specs=[blk, blk, vec, vec],
        out_specs=[blk, blk],
        out_shape=[jax.ShapeDtypeStruct((s, d), F32), jax.ShapeDtypeStruct((s, d), BF16)],
        compiler_params=_params("parallel"),
        name="residual_norm",
    )(y, x, _row(g_post), _row(g_next))


def _mm_kernel(a_ref, w_ref, o_ref):
    o_ref[...] = jnp.dot(a_ref[...], w_ref[...], preferred_element_type=F32).astype(o_ref.dtype)


def _mm_nt_kernel(a_ref, wt_ref, o_ref):
    o_ref[...] = lax.dot_general(a_ref[...], wt_ref[...], (((1,), (1,)), ((), ())),
                                 preferred_element_type=F32).astype(o_ref.dtype)


def matmul_nt(a, wt, layer, *, tm=1024, tn=1024, name="matmul_nt"):
    m, kdim = a.shape
    n = wt.shape[1]
    tm = _tile(m, tm)
    tn = _tile(n, tn)
    return pl.pallas_call(
        _mm_nt_kernel,
        grid=(m // tm, n // tn),
        in_specs=[pl.BlockSpec((tm, kdim), lambda i, j: (i, 0)),
                  pl.BlockSpec((None, tn, kdim), lambda i, j: (layer, j, 0))],
        out_specs=pl.BlockSpec((tm, tn), lambda i, j: (i, j)),
        out_shape=jax.ShapeDtypeStruct((m, n), BF16),
        compiler_params=_params("parallel", "parallel"),
        name=name,
    )(a, wt)


def _mm_acc_kernel(a_ref, w_ref, o_ref, acc_ref):
    k = pl.program_id(2)

    @pl.when(k == 0)
    def _():
        acc_ref[...] = jnp.zeros_like(acc_ref)

    acc_ref[...] += jnp.dot(a_ref[...], w_ref[...], preferred_element_type=F32)

    @pl.when(k == pl.num_programs(2) - 1)
    def _():
        o_ref[...] = acc_ref[...].astype(o_ref.dtype)


def matmul(a, w, layer, *, tm=1024, tn=1024, tk=None, out_dtype=BF16, name="matmul"):
    m, kdim = a.shape
    n = w.shape[2]
    tm = _tile(m, tm)
    tn = _tile(n, tn)
    if tk is None or tk >= kdim:
        return pl.pallas_call(
            _mm_kernel,
            grid=(m // tm, n // tn),
            in_specs=[pl.BlockSpec((tm, kdim), lambda i, j: (i, 0)),
                      pl.BlockSpec((None, kdim, tn), lambda i, j: (layer, 0, j))],
            out_specs=pl.BlockSpec((tm, tn), lambda i, j: (i, j)),
            out_shape=jax.ShapeDtypeStruct((m, n), out_dtype),
            compiler_params=_params("parallel", "parallel"),
            name=name,
        )(a, w)
    assert kdim % tk == 0
    return pl.pallas_call(
        _mm_acc_kernel,
        grid=(m // tm, n // tn, kdim // tk),
        in_specs=[pl.BlockSpec((tm, tk), lambda i, j, k: (i, k)),
                  pl.BlockSpec((None, tk, tn), lambda i, j, k: (layer, k, j))],
        out_specs=pl.BlockSpec((tm, tn), lambda i, j, k: (i, j)),
        out_shape=jax.ShapeDtypeStruct((m, n), out_dtype),
        scratch_shapes=[pltpu.VMEM((tm, tn), F32)],
        compiler_params=_params("parallel", "parallel", "arbitrary"),
        name=name,
    )(a, w)


def _dft_split(s):
    a = 1 << ((s.bit_length() - 1 + 1) // 2)
    assert s % a == 0
    return a, s // a


def _fourier_tables(s):
    a, b = _dft_split(s)
    two_pi = 2.0 * math.pi

    def cs(idx, period):
        ang = (idx % period).astype(F32) * (two_pi / period)
        return jnp.cos(ang), jnp.sin(ang)

    k1 = jnp.arange(a, dtype=jnp.int32)
    c1, s1 = cs(k1[:, None] * k1[None, :], a)
    w1 = jnp.concatenate([c1, -s1], axis=0) * (1.0 / math.sqrt(s))
    k2 = jnp.arange(b, dtype=jnp.int32)
    kk = k1[:, None, None] + a * k2[None, :, None]
    c2, s2 = cs(kk * k2[None, None, :], s)
    g = jnp.concatenate([jnp.concatenate([c2, s2], axis=2),
                         jnp.concatenate([-s2, c2], axis=2)], axis=1)
    c = jnp.arange(F_GROUP_DIM, dtype=jnp.int32)
    cc, sc = cs(c[:, None] * c[None, :], F_GROUP_DIM)
    wc = jnp.concatenate([cc, sc], axis=0) * (1.0 / math.sqrt(F_GROUP_DIM))
    return w1.astype(BF16), g.astype(BF16), wc.astype(BF16)


def _dft1_kernel(w_ref, z_ref, o_ref):
    o_ref[...] = jnp.dot(w_ref[...], z_ref[...], preferred_element_type=F32).astype(o_ref.dtype)


def _dft2_kernel(g_ref, t_ref, wc_ref, o_ref, *, nb, bdim):
    xr, xi = [], []
    for j in range(nb):
        t = t_ref[:, j].reshape(2 * bdim, F_WIDTH)
        x = jnp.dot(g_ref[j], t, preferred_element_type=F32)
        xr.append(x[:bdim].astype(BF16))
        xi.append(x[bdim:].astype(BF16))
    xr = jnp.concatenate(xr, axis=0)
    xi = jnp.concatenate(xi, axis=0)
    for grp in range(F_GROUPS):
        sl = slice(grp * F_GROUP_DIM, (grp + 1) * F_GROUP_DIM)
        y = (jnp.dot(xr[:, sl], wc_ref[:F_GROUP_DIM], preferred_element_type=F32)
             + jnp.dot(xi[:, sl], wc_ref[F_GROUP_DIM:], preferred_element_type=F32))
        for j in range(nb):
            o_ref[:, j * F_WIDTH + grp * F_GROUP_DIM: j * F_WIDTH + (grp + 1) * F_GROUP_DIM] = (
                y[j * bdim:(j + 1) * bdim].astype(o_ref.dtype))


def fourier_mix(zf, tables):
    s = zf.shape[0]
    a, b = _dft_split(s)
    w1, g, wc = tables
    cols = b * F_WIDTH
    tn = _tile(cols, 8 * F_WIDTH)
    t = pl.pallas_call(
        _dft1_kernel,
        grid=(cols // tn,),
        in_specs=[pl.BlockSpec((2 * a, a), lambda j: (0, 0)),
                  pl.BlockSpec((a, tn), lambda j: (0, j))],
        out_specs=pl.BlockSpec((2 * a, tn), lambda j: (0, j)),
        out_shape=jax.ShapeDtypeStruct((2 * a, cols), BF16),
        compiler_params=_params("parallel"),
        name="dft_stage1",
    )(w1, zf.reshape(a, cols))
    nb = _tile(a, 8)
    out = pl.pallas_call(
        functools.partial(_dft2_kernel, nb=nb, bdim=b),
        grid=(a // nb,),
        in_specs=[pl.BlockSpec((nb, 2 * b, 2 * b), lambda i: (i, 0, 0)),
                  pl.BlockSpec((2, nb, b, F_WIDTH), lambda i: (0, i, 0, 0)),
                  pl.BlockSpec((2 * F_GROUP_DIM, F_GROUP_DIM), lambda i: (0, 0))],
        out_specs=pl.BlockSpec((b, nb * F_WIDTH), lambda i: (0, i)),
        out_shape=jax.ShapeDtypeStruct((b, a * F_WIDTH), BF16),
        compiler_params=_params("parallel"),
        name="dft_stage2",
    )(g, t.reshape(2, a, b, F_WIDTH), wc)
    return out.reshape(s, F_WIDTH)


def _rope_lanes(y2, cs):
    w = y2 * cs
    return w + pltpu.roll(w, QK_ROPE, axis=1)


def _mla_q_kernel(cq_ref, gq_ref, cs_ref, w_ref, q_ref, *, scale):
    c = cq_ref[...].astype(F32)
    n = (c * lax.rsqrt(jnp.mean(c * c, axis=-1, keepdims=True) + EPS) * gq_ref[...]).astype(BF16)
    cs = cs_ref[...]
    for hd in range(MLA_HEADS):
        y = jnp.dot(n, w_ref[hd], preferred_element_type=F32) * scale
        q_ref[hd, :, :QK_NOPE] = y[:, :QK_NOPE].astype(q_ref.dtype)
        q_ref[hd, :, QK_NOPE:] = _rope_lanes(y[:, QK_NOPE:], cs).astype(q_ref.dtype)


def _mla_kv_kernel(ckv_ref, kr_ref, gkv_ref, cs_ref, w_ref, k_ref, v_ref):
    c = ckv_ref[...].astype(F32)
    n = (c * lax.rsqrt(jnp.mean(c * c, axis=-1, keepdims=True) + EPS) * gkv_ref[...]).astype(BF16)
    r = _rope_lanes(kr_ref[...].astype(F32), cs_ref[...])
    lane = lax.broadcasted_iota(jnp.int32, r.shape, 1)
    r = jnp.where(lane < QK_ROPE, r, 0.0).astype(k_ref.dtype)
    ones = jnp.ones((v_ref.shape[1], V_DIM), v_ref.dtype)
    for hd in range(MLA_HEADS):
        y = jnp.dot(n, w_ref[hd], preferred_element_type=F32)
        k_ref[hd, :, :QK_NOPE] = y[:, :QK_NOPE].astype(k_ref.dtype)
        k_ref[hd, :, QK_NOPE:] = r
        v_ref[hd, :, :V_DIM] = y[:, QK_NOPE:].astype(v_ref.dtype)
        v_ref[hd, :, V_DIM:] = ones


def _flash_kernel(q_ref, k_ref, v_ref, o_ref, sa_ref, sb_ref, m_ref, acc_ref, *, tk):
    nk = k_ref.shape[0] // tk
    q = q_ref[...]

    def scores(c):
        off = pl.multiple_of(c * tk, tk)
        return lax.dot_general(q, k_ref[pl.ds(off, tk), :], (((1,), (1,)), ((), ())),
                               preferred_element_type=F32)

    def accumulate(c, s_ref):
        off = pl.multiple_of(c * tk, tk)
        s = s_ref[...]
        m_prev = m_ref[...]
        m_new = jnp.maximum(m_prev, jnp.max(s, axis=-1, keepdims=True))
        p = jnp.exp2(s - m_new).astype(BF16)
        acc_ref[...] = (jnp.exp2(m_prev - m_new) * acc_ref[...]
                        + jnp.dot(p, v_ref[pl.ds(off, tk), :], preferred_element_type=F32))
        m_ref[...] = m_new

    m_ref[...] = jnp.full_like(m_ref, -jnp.inf)
    acc_ref[...] = jnp.zeros_like(acc_ref)
    sa_ref[...] = scores(0)

    bufs = (sa_ref, sb_ref)
    trips = (nk - 1) // FLASH_UNROLL

    def body(t, carry):
        a = FLASH_UNROLL * t
        for u in range(FLASH_UNROLL):
            bufs[(u + 1) % 2][...] = scores(a + u + 1)
            accumulate(a + u, bufs[u % 2])
        return carry

    lax.fori_loop(0, trips, body, 0)
    for c in range(trips * FLASH_UNROLL, nk):
        if c + 1 < nk:
            bufs[(c + 1) % 2][...] = scores(c + 1)
        accumulate(c, bufs[c % 2])
    acc = acc_ref[...]
    o_ref[...] = (acc[:, :V_DIM] / acc[:, V_DIM:]).astype(o_ref.dtype)


def mla_mix(cq, ckvr, cs, g_q, wq, g_kv, wkv):
    s = cq.shape[0]
    tm = _tile(s, 1024)
    scale = math.log2(math.e) / math.sqrt(QK_NOPE + QK_ROPE)
    dq = QK_NOPE + 2 * QK_ROPE
    dv = 2 * V_DIM
    q = pl.pallas_call(
        functools.partial(_mla_q_kernel, scale=scale),
        grid=(s // tm,),
        in_specs=[pl.BlockSpec((tm, Q_LORA), lambda i: (i, 0)),
                  pl.BlockSpec((1, Q_LORA), lambda i: (0, 0)),
                  pl.BlockSpec((tm, LANES), lambda i: (i, 0)),
                  pl.BlockSpec((MLA_HEADS, Q_LORA, dq), lambda i: (0, 0, 0))],
        out_specs=pl.BlockSpec((MLA_HEADS, tm, dq), lambda i: (0, i, 0)),
        out_shape=jax.ShapeDtypeStruct((MLA_HEADS, s, dq), BF16),
        compiler_params=_params("parallel"),
        name="mla_q",
    )(cq, _row(g_q), cs, wq)
    k, v = pl.pallas_call(
        _mla_kv_kernel,
        grid=(s // tm,),
        in_specs=[pl.BlockSpec((tm, KV_LORA), lambda i: (i, 0)),
                  pl.BlockSpec((tm, LANES), lambda i: (i, KV_LORA // LANES)),
                  pl.BlockSpec((1, KV_LORA), lambda i: (0, 0)),
                  pl.BlockSpec((tm, LANES), lambda i: (i, 0)),
                  pl.BlockSpec((MLA_HEADS, KV_LORA, QK_NOPE + V_DIM), lambda i: (0, 0, 0))],
        out_specs=[pl.BlockSpec((MLA_HEADS, tm, dq), lambda i: (0, i, 0)),
                   pl.BlockSpec((MLA_HEADS, tm, dv), lambda i: (0, i, 0))],
        out_shape=[jax.ShapeDtypeStruct((MLA_HEADS, s, dq), BF16),
                   jax.ShapeDtypeStruct((MLA_HEADS, s, dv), BF16)],
        compiler_params=_params("parallel"),
        name="mla_kv",
    )(ckvr, ckvr, _row(g_kv), cs, wkv)
    tq = _tile(s, 1024)
    tk = _tile(s, 1024)
    once = pl.Buffered(1)
    return pl.pallas_call(
        functools.partial(_flash_kernel, tk=tk),
        grid=(MLA_HEADS, s // tq),
        in_specs=[pl.BlockSpec((None, tq, dq), lambda h, i: (h, i, 0)),
                  pl.BlockSpec((None, s, dq), lambda h, i: (h, 0, 0), pipeline_mode=once),
                  pl.BlockSpec((None, s, dv), lambda h, i: (h, 0, 0), pipeline_mode=once)],
        out_specs=pl.BlockSpec((tq, V_DIM), lambda h, i: (i, h)),
        out_shape=jax.ShapeDtypeStruct((s, MLA_HEADS * V_DIM), BF16),
        scratch_shapes=[pltpu.VMEM((tq, tk), F32), pltpu.VMEM((tq, tk), F32),
                        pltpu.VMEM((tq, 1), F32), pltpu.VMEM((tq, dv), F32)],
        compiler_params=_params("parallel", "arbitrary"),
        name="mla_flash",
    )(q, k, v)


def _gelu(x):
    return jax.nn.gelu(x, approximate=True)


def _sg_kernel(u_ref, v_ref, g_ref, w_ref, b_ref, o_ref):
    def group(grp, carry):
        cols = pl.ds(pl.multiple_of(grp * SG_GROUP_DIM, SG_GROUP_DIM), SG_GROUP_DIM)
        v = _gelu(v_ref[:, cols].astype(F32))
        mu = jnp.mean(v, axis=-1, keepdims=True)
        vc = v - mu
        vn = (vc * lax.rsqrt(jnp.mean(vc * vc, axis=-1, keepdims=True) + EPS)
              * g_ref[:, cols]).astype(BF16)
        w = w_ref[grp]
        b = b_ref[grp]
        for c in range(u_ref.shape[0] // CHUNK):
            rows = slice(c * CHUNK, (c + 1) * CHUNK)
            sp = jnp.dot(w, vn[rows], preferred_element_type=F32) + b
            o_ref[rows, cols] = (_gelu(u_ref[rows, cols].astype(F32)) * sp).astype(o_ref.dtype)
        return carry

    lax.fori_loop(0, SG_GROUPS, group, 0)


def spatial_gating(zu, zv, ln_gain, w_spatial, b_spatial):
    s = zu.shape[0]
    tm = _tile(s, 1024)
    blk = pl.BlockSpec((tm, SG_WIDTH), lambda i: (i, 0))
    return pl.pallas_call(
        _sg_kernel,
        grid=(s // tm,),
        in_specs=[blk, blk,
                  pl.BlockSpec((1, SG_WIDTH), lambda i: (0, 0)),
                  pl.BlockSpec((SG_GROUPS, CHUNK, CHUNK), lambda i: (0, 0, 0)),
                  pl.BlockSpec((SG_GROUPS, CHUNK, 1), lambda i: (0, 0, 0))],
        out_specs=blk,
        out_shape=jax.ShapeDtypeStruct((s, SG_WIDTH), BF16),
        compiler_params=_params("parallel"),
        name="spatial_gating",
    )(zu, zv, _row(ln_gain), w_spatial.astype(BF16), b_spatial.astype(F32)[:, :, None])


def _merge_kernel(h_ref, f_ref, a_ref, s_ref, wg0_ref, wg1_ref, wg2_ref, b0_ref, b1_ref, b2_ref,
                  wf_ref, wa_ref, ws_ref, o_ref):
    h = h_ref[...]

    def gated(wg_ref, b_ref, y_ref, w_ref):
        gate = jax.nn.sigmoid(jnp.dot(h, wg_ref[...], preferred_element_type=F32) + b_ref[...])
        return gate * jnp.dot(y_ref[...], w_ref[...], preferred_element_type=F32)

    o_ref[...] = (gated(wg0_ref, b0_ref, f_ref, wf_ref) + gated(wg1_ref, b1_ref, a_ref, wa_ref)
                  + gated(wg2_ref, b2_ref, s_ref, ws_ref)).astype(o_ref.dtype)


def gated_merge(h, yf, ya, ys, w_gate, b_gate, w_f, w_a, w_s, layer):
    s, d = h.shape
    tm = _tile(s, 1024)
    tn = _tile(d, 256)
    nj = d // tn
    bg = _row(b_gate)

    def rows(width):
        return pl.BlockSpec((tm, width), lambda i, j: (i, 0))

    def gcol(b):
        return pl.BlockSpec((None, d, tn), lambda i, j, b=b: (layer, 0, j + b * nj))

    def bcol(b):
        return pl.BlockSpec((1, tn), lambda i, j, b=b: (0, j + b * nj))

    def wcol(width):
        return pl.BlockSpec((None, width, tn), lambda i, j: (layer, 0, j))

    return pl.pallas_call(
        _merge_kernel,
        grid=(s // tm, nj),
        in_specs=[rows(d), rows(F_WIDTH), rows(MLA_HEADS * V_DIM), rows(SG_WIDTH),
                  gcol(0), gcol(1), gcol(2), bcol(0), bcol(1), bcol(2),
                  wcol(F_WIDTH), wcol(MLA_HEADS * V_DIM), wcol(SG_WIDTH)],
        out_specs=pl.BlockSpec((tm, tn), lambda i, j: (i, j)),
        out_shape=jax.ShapeDtypeStruct((s, d), BF16),
        compiler_params=_params("parallel", "parallel"),
        name="gated_merge",
    )(h, yf, ya, ys, w_gate, w_gate, w_gate, bg, bg, bg, w_f, w_a, w_s)


def _cross_attn_kernel(q_ref, k_ref, v_ref, o_ref, *, scale):
    for hd in range(MEM_HEADS):
        sl = slice(hd * MEM_HEAD_DIM, (hd + 1) * MEM_HEAD_DIM)
        s = lax.dot_general(q_ref[:, sl], k_ref[:, sl], (((1,), (1,)), ((), ())),
                            preferred_element_type=F32) * scale
        p = jnp.exp(s - jnp.max(s, axis=-1, keepdims=True))
        o = jnp.dot(p.astype(BF16), v_ref[:, sl], preferred_element_type=F32)
        o_ref[:, sl] = (o / jnp.sum(p, axis=-1, keepdims=True)).astype(o_ref.dtype)


def cross_attention(q, k, v):
    s, width = q.shape
    m = k.shape[0]
    tm = _tile(s, 1024)
    return pl.pallas_call(
        functools.partial(_cross_attn_kernel, scale=1.0 / math.sqrt(MEM_HEAD_DIM)),
        grid=(s // tm,),
        in_specs=[pl.BlockSpec((tm, width), lambda i: (i, 0)),
                  pl.BlockSpec((m, width), lambda i: (0, 0)),
                  pl.BlockSpec((m, width), lambda i: (0, 0))],
        out_specs=pl.BlockSpec((tm, width), lambda i: (i, 0)),
        out_shape=jax.ShapeDtypeStruct((s, width), BF16),
        compiler_params=_params("parallel"),
        name="cross_attention",
    )(q, k, v)


def _ffn_up_kernel(h_ref, hp_ref, hn_ref, wg_ref, wv_ref, cwg_ref, cwv_ref, cbg_ref, cbv_ref, o_ref):
    i = pl.program_id(0)
    tm = h_ref.shape[0]
    h = h_ref[...]
    hp = jnp.where(i > 0, hp_ref[...], jnp.zeros_like(hp_ref))
    hn = jnp.where(i < pl.num_programs(0) - 1, hn_ref[...], jnp.zeros_like(hn_ref))
    row = lax.broadcasted_iota(jnp.int32, (tm, 1), 0)

    def conv(w_ref, cw_ref, cb_ref):
        w = w_ref[...].astype(BF16)
        u = jnp.dot(h, w, preferred_element_type=F32)
        up = jnp.dot(hp, w, preferred_element_type=F32)[7:8]
        un = jnp.dot(hn, w, preferred_element_type=F32)[0:1]
        above = jnp.where(row == 0, up, pltpu.roll(u, 1, axis=0))
        below = jnp.where(row == tm - 1, un, pltpu.roll(u, tm - 1, axis=0))
        cw = cw_ref[...]
        return above * cw[0:1] + u * cw[1:2] + below * cw[2:3] + cb_ref[...]

    o_ref[...] = (_gelu(conv(wg_ref, cwg_ref, cbg_ref)) * conv(wv_ref, cwv_ref, cbv_ref)).astype(o_ref.dtype)


def ffn_up(h, w_up, conv_w, conv_b, layer):
    s, d = h.shape
    d_ff = w_up.shape[2] // 2
    tm = _tile(s, 1024)
    tn = _tile(d_ff, 256)
    nj = d_ff // tn
    hb = tm // 8
    last = s // 8 - 1
    cb = _row(conv_b)
    cw = conv_w.astype(F32)
    return pl.pallas_call(
        _ffn_up_kernel,
        grid=(s // tm, nj),
        in_specs=[pl.BlockSpec((tm, d), lambda i, j: (i, 0)),
                  pl.BlockSpec((8, d), lambda i, j: (jnp.maximum(i * hb - 1, 0), 0)),
                  pl.BlockSpec((8, d), lambda i, j: (jnp.minimum((i + 1) * hb, last), 0)),
                  pl.BlockSpec((None, d, tn), lambda i, j: (layer, 0, j)),
                  pl.BlockSpec((None, d, tn), lambda i, j: (layer, 0, j + nj)),
                  pl.BlockSpec((3, tn), lambda i, j: (0, j)),
                  pl.BlockSpec((3, tn), lambda i, j: (0, j + nj)),
                  pl.BlockSpec((1, tn), lambda i, j: (0, j)),
                  pl.BlockSpec((1, tn), lambda i, j: (0, j + nj))],
        out_specs=pl.BlockSpec((tm, tn), lambda i, j: (i, j)),
        out_shape=jax.ShapeDtypeStruct((s, d_ff), BF16),
        compiler_params=_params("parallel", "parallel"),
        name="ffn_up_conv",
    )(h, h, h, w_up, w_up, cw, cw, cb, cb)


def _split_w_in(w_in):
    half = QK_ROPE // 2
    wt = jnp.swapaxes(w_in, 1, 2)
    w_kr = wt[:, OFF_KR:OFF_SG]
    w_kr_rot = jnp.concatenate([-w_kr[:, half:], w_kr[:, :half]], axis=1)
    return (wt[:, :OFF_CQ], wt[:, OFF_CQ:OFF_CKV],
            jnp.concatenate([wt[:, OFF_CKV:OFF_KR], w_kr, w_kr_rot], axis=1),
            wt[:, OFF_SG:OFF_SG + SG_WIDTH], wt[:, OFF_SG + SG_WIDTH:])


def _prep_w_uq(w_uq):
    half = QK_ROPE // 2
    w = w_uq.reshape(Q_LORA, MLA_HEADS, QK_NOPE + QK_ROPE)
    rope = w[:, :, QK_NOPE:]
    rot = jnp.concatenate([-rope[:, :, half:], rope[:, :, :half]], axis=2)
    return jnp.concatenate([w, rot], axis=2).transpose(1, 0, 2).astype(BF16)


def _prep_w_ukv(w_ukv):
    return w_ukv.reshape(KV_LORA, MLA_HEADS, QK_NOPE + V_DIM).transpose(1, 0, 2).astype(BF16)


def kernel(x, mem, positions, mix_pre_norm, mix_post_norm, w_in, mla_q_norm, w_uq, mla_kv_norm, w_ukv, sg_norm, w_spatial, b_spatial, w_br_f, w_br_a, w_br_s, w_gate, b_gate, w_out, mem_pre_norm, mem_post_norm, mem_kv_norm, w_mq, w_mk, w_mv, w_mo, ffn_pre_norm, ffn_post_norm, w_up, conv_w, conv_b, w_down):
    batch, s, d = x.shape
    depth = w_in.shape[0]

    w_f, w_cq, w_ckvr, w_su, w_sv = [w.astype(BF16) for w in _split_w_in(w_in)]
    (w_gate, w_br_f, w_br_a, w_br_s, w_out, w_mq, w_mk, w_mv, w_mo, w_down) = [
        w.astype(BF16) for w in (w_gate, w_br_f, w_br_a, w_br_s, w_out, w_mq, w_mk, w_mv, w_mo,
                                 w_down)]

    inv_freq = ROPE_THETA ** (-jnp.arange(0, QK_ROPE, 2, dtype=F32) / QK_ROPE)
    tables = _fourier_tables(s)
    outs = []
    for b in range(batch):
        ang = positions[b].astype(F32)[:, None] * inv_freq
        cos, sin = jnp.cos(ang), jnp.sin(ang)
        cs = jnp.concatenate([cos, cos, sin, sin], axis=1)
        xb = x[b]
        mem_b = mem[b]
        h = rmsnorm(xb, mix_pre_norm[0])
        for l in range(depth):
            zf = matmul_nt(h, w_f, l, name="in_fourier")
            cq = matmul_nt(h, w_cq, l, name="in_cq")
            ckvr = matmul_nt(h, w_ckvr, l, name="in_ckv_krope")
            zu = matmul_nt(h, w_su, l, name="in_sg_u")
            zv = matmul_nt(h, w_sv, l, name="in_sg_v")
            yf = fourier_mix(zf, tables)
            ya = mla_mix(cq, ckvr, cs, mla_q_norm[l], _prep_w_uq(w_uq[l]),
                         mla_kv_norm[l], _prep_w_ukv(w_ukv[l]))
            ys = spatial_gating(zu, zv, sg_norm[l], w_spatial[l], b_spatial[l])
            merged = gated_merge(h, yf, ya, ys, w_gate, b_gate[l], w_br_f, w_br_a, w_br_s, l)
            y = matmul(merged, w_out, l, name="mix_out")
            xb, h = residual_norm(y, xb, mix_post_norm[l], mem_pre_norm[l])

            mem_n = rmsnorm(mem_b, mem_kv_norm[l])
            q = matmul(h, w_mq, l, name="mem_q")
            k = matmul(mem_n, w_mk, l, name="mem_k")
            v = matmul(mem_n, w_mv, l, name="mem_v")
            o = cross_attention(q, k, v)
            y = matmul(o, w_mo, l, name="mem_out")
            xb, h = residual_norm(y, xb, mem_post_norm[l], ffn_pre_norm[l])

            act = ffn_up(h, w_up, conv_w[l], conv_b[l], l)
            y = matmul(act, w_down, l, tm=512, tn=512, name="ffn_down")
            g_next = mix_pre_norm[l + 1] if l + 1 < depth else None
            xb, h = residual_norm(y, xb, ffn_post_norm[l], g_next)
        outs.append(xb)
    return outs[0][None] if batch == 1 else jnp.stack(outs, axis=0)
```

```python
import functools
import math

import jax
import jax.numpy as jnp
from jax import lax
from jax.experimental import pallas as pl
from jax.experimental.pallas import tpu as pltpu

F32 = jnp.float32
BF16 = jnp.bfloat16
EPS = 1e-6

V7X_VMEM_BYTES = 64 * 1024 * 1024
VMEM_LIMIT_BYTES = V7X_VMEM_BYTES - 4 * 1024 * 1024
LANES = 128
FLASH_UNROLL = 4
HALO = 16

F_GROUPS = 4
F_GROUP_DIM = 256
F_WIDTH = F_GROUPS * F_GROUP_DIM
MLA_HEADS = 8
QK_NOPE = 128
QK_ROPE = 64
V_DIM = 128
Q_LORA = 768
KV_LORA = 512
ROPE_THETA = 10000.0
SG_GROUPS = 16
SG_GROUP_DIM = 128
SG_WIDTH = SG_GROUPS * SG_GROUP_DIM
CHUNK = 128
MEM_HEADS = 4
MEM_HEAD_DIM = 256
OFF_CQ = F_WIDTH
OFF_CKV = OFF_CQ + Q_LORA
OFF_KR = OFF_CKV + KV_LORA
OFF_SG = OFF_KR + QK_ROPE


def _params(*sem):
    return pltpu.CompilerParams(dimension_semantics=sem, vmem_limit_bytes=VMEM_LIMIT_BYTES)


def _tile(n, pref):
    if n <= pref:
        return n
    t = pref
    while n % t:
        t //= 2
    return t


def _row(v):
    return v.reshape(1, -1).astype(F32)


def _rmsnorm_kernel(x_ref, g_ref, o_ref):
    x = x_ref[...].astype(F32)
    ms = jnp.mean(x * x, axis=-1, keepdims=True)
    o_ref[...] = (x * lax.rsqrt(ms + EPS) * g_ref[...]).astype(o_ref.dtype)


def rmsnorm(x, g):
    s, d = x.shape
    tm = _tile(s, 256)
    return pl.pallas_call(
        _rmsnorm_kernel,
        grid=(s // tm,),
        in_specs=[pl.BlockSpec((tm, d), lambda i: (i, 0)), pl.BlockSpec((1, d), lambda i: (0, 0))],
        out_specs=pl.BlockSpec((tm, d), lambda i: (i, 0)),
        out_shape=jax.ShapeDtypeStruct((s, d), BF16),
        compiler_params=_params("parallel"),
        name="rmsnorm",
    )(x, _row(g))


def _residual_norm_kernel(y_ref, x_ref, gp_ref, gn_ref, xo_ref, ho_ref):
    y = y_ref[...].astype(F32)
    yn = y * lax.rsqrt(jnp.mean(y * y, axis=-1, keepdims=True) + EPS) * gp_ref[...]
    xn = x_ref[...] + yn
    xo_ref[...] = xn
    ho_ref[...] = (xn * lax.rsqrt(jnp.mean(xn * xn, axis=-1, keepdims=True) + EPS)
                   * gn_ref[...]).astype(ho_ref.dtype)


def _residual_only_kernel(y_ref, x_ref, gp_ref, xo_ref):
    y = y_ref[...].astype(F32)
    yn = y * lax.rsqrt(jnp.mean(y * y, axis=-1, keepdims=True) + EPS) * gp_ref[...]
    xo_ref[...] = x_ref[...] + yn


def residual_norm(y, x, g_post, g_next):
    s, d = x.shape
    tm = _tile(s, 256)
    blk = pl.BlockSpec((tm, d), lambda i: (i, 0))
    vec = pl.BlockSpec((1, d), lambda i: (0, 0))
    if g_next is None:
        return pl.pallas_call(
            _residual_only_kernel,
            grid=(s // tm,),
            in_specs=[blk, blk, vec],
            out_specs=blk,
            out_shape=jax.ShapeDtypeStruct((s, d), F32),
            compiler_params=_params("parallel"),
            name="residual_final",
        )(y, x, _row(g_post)), None
    return pl.pallas_call(
        _residual_norm_kernel,
        grid=(s // tm,),
        in_specs=[blk, blk, vec, vec],
        out_specs=[blk, blk],
        out_shape=[jax.ShapeDtypeStruct((s, d), F32), jax.ShapeDtypeStruct((s, d), BF16)],
        compiler_params=_params("parallel"),
        name="residual_norm",
    )(y, x, _row(g_post), _row(g_next))


def _mm_kernel(a_ref, w_ref, o_ref):
    o_ref[...] = jnp.dot(a_ref[...], w_ref[...].astype(BF16),
                         preferred_element_type=F32).astype(o_ref.dtype)


def _mm_nt_kernel(a_ref, wt_ref, o_ref):
    o_ref[...] = lax.dot_general(a_ref[...], wt_ref[...], (((1,), (1,)), ((), ())),
                                 preferred_element_type=F32).astype(o_ref.dtype)


def matmul_nt(a, wt, layer, *, tm=1024, tn=1024, name="matmul_nt"):
    m, kdim = a.shape
    n = wt.shape[1]
    tm = _tile(m, tm)
    tn = _tile(n, tn)
    return pl.pallas_call(
        _mm_nt_kernel,
        grid=(m // tm, n // tn),
        in_specs=[pl.BlockSpec((tm, kdim), lambda i, j: (i, 0)),
                  pl.BlockSpec((None, tn, kdim), lambda i, j: (layer, j, 0))],
        out_specs=pl.BlockSpec((tm, tn), lambda i, j: (i, j)),
        out_shape=jax.ShapeDtypeStruct((m, n), BF16),
        compiler_params=_params("parallel", "parallel"),
        name=name,
    )(a, wt)


def matmul(a, w, layer, *, tm=1024, tn=1024, name="matmul"):
    m, kdim = a.shape
    n = w.shape[2]
    tm = _tile(m, tm)
    tn = _tile(n, tn // (w.dtype.itemsize // 2))
    return pl.pallas_call(
        _mm_kernel,
        grid=(m // tm, n // tn),
        in_specs=[pl.BlockSpec((tm, kdim), lambda i, j: (i, 0)),
                  pl.BlockSpec((None, kdim, tn), lambda i, j: (layer, 0, j))],
        out_specs=pl.BlockSpec((tm, tn), lambda i, j: (i, j)),
        out_shape=jax.ShapeDtypeStruct((m, n), BF16),
        compiler_params=_params("parallel", "parallel"),
        name=name,
    )(a, w)


def _dft_split(s):
    a = 1 << ((s.bit_length() - 1 + 1) // 2)
    assert s % a == 0
    return a, s // a


def _fourier_tables(s):
    a, b = _dft_split(s)
    two_pi = 2.0 * math.pi

    def cs(idx, period):
        ang = (idx % period).astype(F32) * (two_pi / period)
        return jnp.cos(ang), jnp.sin(ang)

    k1 = jnp.arange(a, dtype=jnp.int32)
    c1, s1 = cs(k1[:, None] * k1[None, :], a)
    w1 = jnp.concatenate([c1, -s1], axis=0) * (1.0 / math.sqrt(s))
    k2 = jnp.arange(b, dtype=jnp.int32)
    kk = k1[:, None, None] + a * k2[None, :, None]
    c2, s2 = cs(kk * k2[None, None, :], s)
    g = jnp.concatenate([jnp.concatenate([c2, s2], axis=2),
                         jnp.concatenate([-s2, c2], axis=2)], axis=1)
    c = jnp.arange(F_GROUP_DIM, dtype=jnp.int32)
    cc, sc = cs(c[:, None] * c[None, :], F_GROUP_DIM)
    wc = jnp.concatenate([cc, sc], axis=0) * (1.0 / math.sqrt(F_GROUP_DIM))
    return w1.astype(BF16), g.astype(BF16), wc.astype(BF16)


def _dft1_kernel(w_ref, z_ref, o_ref):
    o_ref[...] = jnp.dot(w_ref[...], z_ref[...], preferred_element_type=F32).astype(o_ref.dtype)


def _dft2_kernel(g_ref, t_ref, wc_ref, o_ref, *, nb, bdim):
    xr, xi = [], []
    for j in range(nb):
        t = t_ref[:, j].reshape(2 * bdim, F_WIDTH)
        x = jnp.dot(g_ref[j], t, preferred_element_type=F32)
        xr.append(x[:bdim].astype(BF16))
        xi.append(x[bdim:].astype(BF16))
    xr = jnp.concatenate(xr, axis=0)
    xi = jnp.concatenate(xi, axis=0)
    for grp in range(F_GROUPS):
        sl = slice(grp * F_GROUP_DIM, (grp + 1) * F_GROUP_DIM)
        y = (jnp.dot(xr[:, sl], wc_ref[:F_GROUP_DIM], preferred_element_type=F32)
             + jnp.dot(xi[:, sl], wc_ref[F_GROUP_DIM:], preferred_element_type=F32))
        for j in range(nb):
            o_ref[:, j * F_WIDTH + grp * F_GROUP_DIM: j * F_WIDTH + (grp + 1) * F_GROUP_DIM] = (
                y[j * bdim:(j + 1) * bdim].astype(o_ref.dtype))


def fourier_mix(zf, tables):
    s = zf.shape[0]
    a, b = _dft_split(s)
    w1, g, wc = tables
    cols = b * F_WIDTH
    tn = _tile(cols, 8 * F_WIDTH)
    t = pl.pallas_call(
        _dft1_kernel,
        grid=(cols // tn,),
        in_specs=[pl.BlockSpec((2 * a, a), lambda j: (0, 0)),
                  pl.BlockSpec((a, tn), lambda j: (0, j))],
        out_specs=pl.BlockSpec((2 * a, tn), lambda j: (0, j)),
        out_shape=jax.ShapeDtypeStruct((2 * a, cols), BF16),
        compiler_params=_params("parallel"),
        name="dft_stage1",
    )(w1, zf.reshape(a, cols))
    nb = _tile(a, 8)
    out = pl.pallas_call(
        functools.partial(_dft2_kernel, nb=nb, bdim=b),
        grid=(a // nb,),
        in_specs=[pl.BlockSpec((nb, 2 * b, 2 * b), lambda i: (i, 0, 0)),
                  pl.BlockSpec((2, nb, b, F_WIDTH), lambda i: (0, i, 0, 0)),
                  pl.BlockSpec((2 * F_GROUP_DIM, F_GROUP_DIM), lambda i: (0, 0))],
        out_specs=pl.BlockSpec((b, nb * F_WIDTH), lambda i: (0, i)),
        out_shape=jax.ShapeDtypeStruct((b, a * F_WIDTH), BF16),
        compiler_params=_params("parallel"),
        name="dft_stage2",
    )(g, t.reshape(2, a, b, F_WIDTH), wc)
    return out.reshape(s, F_WIDTH)


def _rope_lanes(y2, cs):
    w = y2 * cs
    return w + pltpu.roll(w, QK_ROPE, axis=1)


def _mla_q_kernel(cq_ref, gq_ref, cs_ref, w_ref, q_ref, *, scale):
    c = cq_ref[...].astype(F32)
    n = (c * lax.rsqrt(jnp.mean(c * c, axis=-1, keepdims=True) + EPS) * gq_ref[...]).astype(BF16)
    cs = cs_ref[...]
    for hd in range(MLA_HEADS):
        y = jnp.dot(n, w_ref[hd], preferred_element_type=F32) * scale
        q_ref[hd, :, :QK_NOPE] = y[:, :QK_NOPE].astype(q_ref.dtype)
        q_ref[hd, :, QK_NOPE:] = _rope_lanes(y[:, QK_NOPE:], cs).astype(q_ref.dtype)


def _mla_kv_kernel(ckv_ref, kr_ref, gkv_ref, cs_ref, w_ref, k_ref, v_ref):
    c = ckv_ref[...].astype(F32)
    n = (c * lax.rsqrt(jnp.mean(c * c, axis=-1, keepdims=True) + EPS) * gkv_ref[...]).astype(BF16)
    r = _rope_lanes(kr_ref[...].astype(F32), cs_ref[...])
    lane = lax.broadcasted_iota(jnp.int32, r.shape, 1)
    r = jnp.where(lane < QK_ROPE, r, 0.0).astype(k_ref.dtype)
    ones = jnp.ones((v_ref.shape[1], V_DIM), v_ref.dtype)
    for hd in range(MLA_HEADS):
        y = jnp.dot(n, w_ref[hd], preferred_element_type=F32)
        k_ref[hd, :, :QK_NOPE] = y[:, :QK_NOPE].astype(k_ref.dtype)
        k_ref[hd, :, QK_NOPE:] = r
        v_ref[hd, :, :V_DIM] = y[:, QK_NOPE:].astype(v_ref.dtype)
        v_ref[hd, :, V_DIM:] = ones


def _flash_kernel(q_ref, k_ref, v_ref, o_ref, sa_ref, sb_ref, m_ref, acc_ref, *, tk):
    nk = k_ref.shape[0] // tk
    q = q_ref[...]

    def scores(c):
        off = pl.multiple_of(c * tk, tk)
        return lax.dot_general(q, k_ref[pl.ds(off, tk), :], (((1,), (1,)), ((), ())),
                               preferred_element_type=F32)

    def accumulate(c, s_ref):
        off = pl.multiple_of(c * tk, tk)
        s = s_ref[...]
        m_prev = m_ref[...]
        m_new = jnp.maximum(m_prev, jnp.max(s, axis=-1, keepdims=True))
        p = jnp.exp2(s - m_new).astype(BF16)
        acc_ref[...] = (jnp.exp2(m_prev - m_new) * acc_ref[...]
                        + jnp.dot(p, v_ref[pl.ds(off, tk), :], preferred_element_type=F32))
        m_ref[...] = m_new

    m_ref[...] = jnp.full_like(m_ref, -jnp.inf)
    acc_ref[...] = jnp.zeros_like(acc_ref)
    sa_ref[...] = scores(0)

    bufs = (sa_ref, sb_ref)
    trips = (nk - 1) // FLASH_UNROLL

    def body(t, carry):
        a = FLASH_UNROLL * t
        for u in range(FLASH_UNROLL):
            bufs[(u + 1) % 2][...] = scores(a + u + 1)
            accumulate(a + u, bufs[u % 2])
        return carry

    lax.fori_loop(0, trips, body, 0)
    for c in range(trips * FLASH_UNROLL, nk):
        if c + 1 < nk:
            bufs[(c + 1) % 2][...] = scores(c + 1)
        accumulate(c, bufs[c % 2])
    acc = acc_ref[...]
    o_ref[...] = (acc[:, :V_DIM] / acc[:, V_DIM:]).astype(o_ref.dtype)


def mla_mix(cq, ckvr, cs, g_q, wq, g_kv, wkv):
    s = cq.shape[0]
    tm = _tile(s, 1024)
    scale = math.log2(math.e) / math.sqrt(QK_NOPE + QK_ROPE)
    dq = QK_NOPE + 2 * QK_ROPE
    dv = 2 * V_DIM
    q = pl.pallas_call(
        functools.partial(_mla_q_kernel, scale=scale),
        grid=(s // tm,),
        in_specs=[pl.BlockSpec((tm, Q_LORA), lambda i: (i, 0)),
                  pl.BlockSpec((1, Q_LORA), lambda i: (0, 0)),
                  pl.BlockSpec((tm, LANES), lambda i: (i, 0)),
                  pl.BlockSpec((MLA_HEADS, Q_LORA, dq), lambda i: (0, 0, 0))],
        out_specs=pl.BlockSpec((MLA_HEADS, tm, dq), lambda i: (0, i, 0)),
        out_shape=jax.ShapeDtypeStruct((MLA_HEADS, s, dq), BF16),
        compiler_params=_params("parallel"),
        name="mla_q",
    )(cq, _row(g_q), cs, wq)
    k, v = pl.pallas_call(
        _mla_kv_kernel,
        grid=(s // tm,),
        in_specs=[pl.BlockSpec((tm, KV_LORA), lambda i: (i, 0)),
                  pl.BlockSpec((tm, LANES), lambda i: (i, KV_LORA // LANES)),
                  pl.BlockSpec((1, KV_LORA), lambda i: (0, 0)),
                  pl.BlockSpec((tm, LANES), lambda i: (i, 0)),
                  pl.BlockSpec((MLA_HEADS, KV_LORA, QK_NOPE + V_DIM), lambda i: (0, 0, 0))],
        out_specs=[pl.BlockSpec((MLA_HEADS, tm, dq), lambda i: (0, i, 0)),
                   pl.BlockSpec((MLA_HEADS, tm, dv), lambda i: (0, i, 0))],
        out_shape=[jax.ShapeDtypeStruct((MLA_HEADS, s, dq), BF16),
                   jax.ShapeDtypeStruct((MLA_HEADS, s, dv), BF16)],
        compiler_params=_params("parallel"),
        name="mla_kv",
    )(ckvr, ckvr, _row(g_kv), cs, wkv)
    tq = _tile(s, 1024)
    tk = _tile(s, 1024)
    return pl.pallas_call(
        functools.partial(_flash_kernel, tk=tk),
        grid=(MLA_HEADS, s // tq),
        in_specs=[pl.BlockSpec((None, tq, dq), lambda h, i: (h, i, 0)),
                  pl.BlockSpec((None, s, dq), lambda h, i: (h, 0, 0)),
                  pl.BlockSpec((None, s, dv), lambda h, i: (h, 0, 0))],
        out_specs=pl.BlockSpec((tq, V_DIM), lambda h, i: (i, h)),
        out_shape=jax.ShapeDtypeStruct((s, MLA_HEADS * V_DIM), BF16),
        scratch_shapes=[pltpu.VMEM((tq, tk), F32), pltpu.VMEM((tq, tk), F32),
                        pltpu.VMEM((tq, 1), F32), pltpu.VMEM((tq, dv), F32)],
        compiler_params=_params("parallel", "arbitrary"),
        name="mla_flash",
    )(q, k, v)


GELU_C = math.sqrt(2.0 / math.pi)


def _gelu_tanh(x):
    return jnp.tanh(x * ((x * x) * (GELU_C * 0.044715) + GELU_C))


def _sg_kernel(u_ref, v_ref, g_ref, w_ref, b_ref, o_ref):
    def group(grp, carry):
        cols = pl.ds(pl.multiple_of(grp * SG_GROUP_DIM, SG_GROUP_DIM), SG_GROUP_DIM)
        v = v_ref[:, cols].astype(F32)
        hv = 0.5 * v
        v = hv + hv * _gelu_tanh(v)
        mu = jnp.mean(v, axis=-1, keepdims=True)
        vc = v - mu
        vn = (vc * lax.rsqrt(jnp.mean(vc * vc, axis=-1, keepdims=True) + EPS)
              * g_ref[:, cols]).astype(BF16)
        w = w_ref[grp]
        b = b_ref[grp]
        for c in range(u_ref.shape[0] // CHUNK):
            rows = slice(c * CHUNK, (c + 1) * CHUNK)
            sp_half = jnp.dot(w, vn[rows], preferred_element_type=F32) + b
            u = u_ref[rows, cols].astype(F32)
            o_ref[rows, cols] = ((u * sp_half) * (1.0 + _gelu_tanh(u))).astype(o_ref.dtype)
        return carry

    lax.fori_loop(0, SG_GROUPS, group, 0)


def spatial_gating(zu, zv, ln_gain, w_spatial, b_spatial):
    s = zu.shape[0]
    tm = _tile(s, 1024)
    blk = pl.BlockSpec((tm, SG_WIDTH), lambda i: (i, 0))
    return pl.pallas_call(
        _sg_kernel,
        grid=(s // tm,),
        in_specs=[blk, blk,
                  pl.BlockSpec((1, SG_WIDTH), lambda i: (0, 0)),
                  pl.BlockSpec((SG_GROUPS, CHUNK, CHUNK), lambda i: (0, 0, 0)),
                  pl.BlockSpec((SG_GROUPS, CHUNK, 1), lambda i: (0, 0, 0))],
        out_specs=blk,
        out_shape=jax.ShapeDtypeStruct((s, SG_WIDTH), BF16),
        compiler_params=_params("parallel"),
        name="spatial_gating",
    )(zu, zv, _row(ln_gain), (0.5 * w_spatial).astype(BF16), 0.5 * b_spatial.astype(F32)[:, :, None])


def _merge_kernel(h_ref, f_ref, a_ref, s_ref, wg0_ref, wg1_ref, wg2_ref, b0_ref, b1_ref, b2_ref,
                  wf_ref, wa_ref, ws_ref, o_ref):
    h = h_ref[...]

    def gated(wg_ref, b_ref, y_ref, w_ref):
        gate = jax.nn.sigmoid(jnp.dot(h, wg_ref[...], preferred_element_type=F32) + b_ref[...])
        return gate * jnp.dot(y_ref[...], w_ref[...], preferred_element_type=F32)

    o_ref[...] = (gated(wg0_ref, b0_ref, f_ref, wf_ref) + gated(wg1_ref, b1_ref, a_ref, wa_ref)
                  + gated(wg2_ref, b2_ref, s_ref, ws_ref)).astype(o_ref.dtype)


def gated_merge(h, yf, ya, ys, w_gate, b_gate, w_f, w_a, w_s, layer):
    s, d = h.shape
    tm = _tile(s, 1024)
    tn = _tile(d, 256)
    nj = d // tn
    bg = _row(b_gate)

    def rows(width):
        return pl.BlockSpec((tm, width), lambda i, j: (i, 0))

    def gcol(b):
        return pl.BlockSpec((None, d, tn), lambda i, j, b=b: (layer, 0, j + b * nj))

    def bcol(b):
        return pl.BlockSpec((1, tn), lambda i, j, b=b: (0, j + b * nj))

    def wcol(width):
        return pl.BlockSpec((None, width, tn), lambda i, j: (layer, 0, j))

    return pl.pallas_call(
        _merge_kernel,
        grid=(s // tm, nj),
        in_specs=[rows(d), rows(F_WIDTH), rows(MLA_HEADS * V_DIM), rows(SG_WIDTH),
                  gcol(0), gcol(1), gcol(2), bcol(0), bcol(1), bcol(2),
                  wcol(F_WIDTH), wcol(MLA_HEADS * V_DIM), wcol(SG_WIDTH)],
        out_specs=pl.BlockSpec((tm, tn), lambda i, j: (i, j)),
        out_shape=jax.ShapeDtypeStruct((s, d), BF16),
        compiler_params=_params("parallel", "parallel"),
        name="gated_merge",
    )(h, yf, ya, ys, w_gate, w_gate, w_gate, bg, bg, bg, w_f, w_a, w_s)


def _cross_attn_kernel(q_ref, k_ref, v_ref, o_ref, *, scale):
    for hd in range(MEM_HEADS):
        sl = slice(hd * MEM_HEAD_DIM, (hd + 1) * MEM_HEAD_DIM)
        s = lax.dot_general(q_ref[:, sl], k_ref[:, sl], (((1,), (1,)), ((), ())),
                            preferred_element_type=F32) * scale
        p = jnp.exp(s - jnp.max(s, axis=-1, keepdims=True))
        o = jnp.dot(p.astype(BF16), v_ref[:, sl], preferred_element_type=F32)
        o_ref[:, sl] = (o / jnp.sum(p, axis=-1, keepdims=True)).astype(o_ref.dtype)


def cross_attention(q, k, v):
    s, width = q.shape
    m = k.shape[0]
    tm = _tile(s, 1024)
    return pl.pallas_call(
        functools.partial(_cross_attn_kernel, scale=1.0 / math.sqrt(MEM_HEAD_DIM)),
        grid=(s // tm,),
        in_specs=[pl.BlockSpec((tm, width), lambda i: (i, 0)),
                  pl.BlockSpec((m, width), lambda i: (0, 0)),
                  pl.BlockSpec((m, width), lambda i: (0, 0))],
        out_specs=pl.BlockSpec((tm, width), lambda i: (i, 0)),
        out_shape=jax.ShapeDtypeStruct((s, width), BF16),
        compiler_params=_params("parallel"),
        name="cross_attention",
    )(q, k, v)


def _ffn_up_kernel(h_ref, hp_ref, hn_ref, wg_ref, wv_ref, cwg_ref, cwv_ref, cbg_ref, cbv_ref, o_ref,
                   hx_ref):
    i = pl.program_id(0)
    tm = h_ref.shape[0]

    @pl.when(pl.program_id(1) == 0)
    def _():
        hx_ref[0:HALO] = jnp.where(i > 0, hp_ref[...], jnp.zeros_like(hp_ref))
        hx_ref[HALO:HALO + tm] = h_ref[...]
        hx_ref[HALO + tm:] = jnp.where(i < pl.num_programs(0) - 1, hn_ref[...], jnp.zeros_like(hn_ref))

    hx = hx_ref[...]
    rows = hx.shape[0]

    def conv(w_ref, cw_ref, cb_ref):
        u = jnp.dot(hx, w_ref[...].astype(BF16), preferred_element_type=F32)
        above = pltpu.roll(u, 1, axis=0)[HALO:HALO + tm]
        below = pltpu.roll(u, rows - 1, axis=0)[HALO:HALO + tm]
        cw = cw_ref[...]
        return above * cw[0:1] + u[HALO:HALO + tm] * cw[1:2] + below * cw[2:3] + cb_ref[...]

    g = conv(wg_ref, cwg_ref, cbg_ref)
    v_half = conv(wv_ref, cwv_ref, cbv_ref)
    o_ref[...] = ((g * v_half) * (1.0 + _gelu_tanh(g))).astype(o_ref.dtype)


def ffn_up(h, w_up, conv_w, conv_b, layer):
    s, d = h.shape
    d_ff = w_up.shape[2] // 2
    tm = _tile(s, 1024)
    tn = _tile(d_ff, 256)
    nj = d_ff // tn
    hb = tm // HALO
    last = s // HALO - 1
    half = jnp.concatenate([jnp.ones((d_ff,), F32), jnp.full((d_ff,), 0.5, F32)])
    cb = _row(conv_b) * half
    cw = conv_w.astype(F32) * half
    return pl.pallas_call(
        _ffn_up_kernel,
        grid=(s // tm, nj),
        in_specs=[pl.BlockSpec((tm, d), lambda i, j: (i, 0)),
                  pl.BlockSpec((HALO, d), lambda i, j: (jnp.maximum(i * hb - 1, 0), 0)),
                  pl.BlockSpec((HALO, d), lambda i, j: (jnp.minimum((i + 1) * hb, last), 0)),
                  pl.BlockSpec((None, d, tn), lambda i, j: (layer, 0, j)),
                  pl.BlockSpec((None, d, tn), lambda i, j: (layer, 0, j + nj)),
                  pl.BlockSpec((3, tn), lambda i, j: (0, j)),
                  pl.BlockSpec((3, tn), lambda i, j: (0, j + nj)),
                  pl.BlockSpec((1, tn), lambda i, j: (0, j)),
                  pl.BlockSpec((1, tn), lambda i, j: (0, j + nj))],
        out_specs=pl.BlockSpec((tm, tn), lambda i, j: (i, j)),
        out_shape=jax.ShapeDtypeStruct((s, d_ff), BF16),
        scratch_shapes=[pltpu.VMEM((tm + 2 * HALO, d), BF16)],
        compiler_params=_params("parallel", "arbitrary"),
        name="ffn_up_conv",
    )(h, h, h, w_up, w_up, cw, cw, cb, cb)


def _split_w_in(w_in):
    half = QK_ROPE // 2
    wt = jnp.swapaxes(w_in, 1, 2)
    w_kr = wt[:, OFF_KR:OFF_SG]
    w_kr_rot = jnp.concatenate([-w_kr[:, half:], w_kr[:, :half]], axis=1)
    return (wt[:, :OFF_CQ], wt[:, OFF_CQ:OFF_CKV],
            jnp.concatenate([wt[:, OFF_CKV:OFF_KR], w_kr, w_kr_rot], axis=1),
            wt[:, OFF_SG:OFF_SG + SG_WIDTH], wt[:, OFF_SG + SG_WIDTH:])


def _prep_w_uq(w_uq):
    half = QK_ROPE // 2
    w = w_uq.reshape(Q_LORA, MLA_HEADS, QK_NOPE + QK_ROPE)
    rope = w[:, :, QK_NOPE:]
    rot = jnp.concatenate([-rope[:, :, half:], rope[:, :, :half]], axis=2)
    return jnp.concatenate([w, rot], axis=2).transpose(1, 0, 2).astype(BF16)


def _prep_w_ukv(w_ukv):
    return w_ukv.reshape(KV_LORA, MLA_HEADS, QK_NOPE + V_DIM).transpose(1, 0, 2).astype(BF16)


def kernel(x, mem, positions, mix_pre_norm, mix_post_norm, w_in, mla_q_norm, w_uq, mla_kv_norm, w_ukv, sg_norm, w_spatial, b_spatial, w_br_f, w_br_a, w_br_s, w_gate, b_gate, w_out, mem_pre_norm, mem_post_norm, mem_kv_norm, w_mq, w_mk, w_mv, w_mo, ffn_pre_norm, ffn_post_norm, w_up, conv_w, conv_b, w_down):
    batch, s, d = x.shape
    depth = w_in.shape[0]

    w_f, w_cq, w_ckvr, w_su, w_sv = [w.astype(BF16) for w in _split_w_in(w_in)]
    w_gate, w_br_f, w_br_a, w_br_s, w_down = [
        w.astype(BF16) for w in (w_gate, w_br_f, w_br_a, w_br_s, w_down)]

    inv_freq = ROPE_THETA ** (-jnp.arange(0, QK_ROPE, 2, dtype=F32) / QK_ROPE)
    tables = _fourier_tables(s)
    outs = []
    for b in range(batch):
        ang = positions[b].astype(F32)[:, None] * inv_freq
        cos, sin = jnp.cos(ang), jnp.sin(ang)
        cs = jnp.concatenate([cos, cos, sin, sin], axis=1)
        xb = x[b]
        mem_b = mem[b]
        h = rmsnorm(xb, mix_pre_norm[0])
        for l in range(depth):
            zf = matmul_nt(h, w_f, l, name="in_fourier")
            cq = matmul_nt(h, w_cq, l, name="in_cq")
            ckvr = matmul_nt(h, w_ckvr, l, name="in_ckv_krope")
            zu = matmul_nt(h, w_su, l, name="in_sg_u")
            zv = matmul_nt(h, w_sv, l, name="in_sg_v")
            yf = fourier_mix(zf, tables)
            ya = mla_mix(cq, ckvr, cs, mla_q_norm[l], _prep_w_uq(w_uq[l]),
                         mla_kv_norm[l], _prep_w_ukv(w_ukv[l]))
            ys = spatial_gating(zu, zv, sg_norm[l], w_spatial[l], b_spatial[l])
            merged = gated_merge(h, yf, ya, ys, w_gate, b_gate[l], w_br_f, w_br_a, w_br_s, l)
            y = matmul(merged, w_out, l, name="mix_out")
            xb, h = residual_norm(y, xb, mix_post_norm[l], mem_pre_norm[l])

            mem_n = rmsnorm(mem_b, mem_kv_norm[l])
            q = matmul(h, w_mq, l, name="mem_q")
            k = matmul(mem_n, w_mk, l, name="mem_k")
            v = matmul(mem_n, w_mv, l, name="mem_v")
            o = cross_attention(q, k, v)
            y = matmul(o, w_mo, l, name="mem_out")
            xb, h = residual_norm(y, xb, mem_post_norm[l], ffn_pre_norm[l])

            act = ffn_up(h, w_up, conv_w[l], conv_b[l], l)
            y = matmul(act, w_down, l, tm=512, tn=512, name="ffn_down")
            g_next = mix_pre_norm[l + 1] if l + 1 < depth else None
            xb, h = residual_norm(y, xb, ffn_post_norm[l], g_next)
        outs.append(xb)
    return outs[0][None] if batch == 1 else jnp.stack(outs, axis=0)
```

```python
import functools
import math

import jax
import jax.numpy as jnp
from jax import lax
from jax.experimental import pallas as pl
from jax.experimental.pallas import tpu as pltpu

F32 = jnp.float32
BF16 = jnp.bfloat16
EPS = 1e-6

V7X_VMEM_BYTES = 64 * 1024 * 1024
VMEM_LIMIT_BYTES = V7X_VMEM_BYTES - 4 * 1024 * 1024
LANES = 128
FLASH_UNROLL = 6
HALO = 16

F_GROUPS = 4
F_GROUP_DIM = 256
F_WIDTH = F_GROUPS * F_GROUP_DIM
MLA_HEADS = 8
QK_NOPE = 128
QK_ROPE = 64
V_DIM = 128
Q_LORA = 768
KV_LORA = 512
ROPE_THETA = 10000.0
SG_GROUPS = 16
SG_GROUP_DIM = 128
SG_WIDTH = SG_GROUPS * SG_GROUP_DIM
CHUNK = 128
MEM_HEADS = 4
MEM_HEAD_DIM = 256
OFF_CQ = F_WIDTH
OFF_CKV = OFF_CQ + Q_LORA
OFF_KR = OFF_CKV + KV_LORA
OFF_SG = OFF_KR + QK_ROPE


def _params(*sem):
    return pltpu.CompilerParams(dimension_semantics=sem, vmem_limit_bytes=VMEM_LIMIT_BYTES)


def _tile(n, pref):
    if n <= pref:
        return n
    t = pref
    while n % t:
        t //= 2
    return t


def _row(v):
    return v.reshape(1, -1).astype(F32)


def _rmsnorm_kernel(x_ref, g_ref, o_ref):
    x = x_ref[...].astype(F32)
    ms = jnp.mean(x * x, axis=-1, keepdims=True)
    o_ref[...] = (x * lax.rsqrt(ms + EPS) * g_ref[...]).astype(o_ref.dtype)


def rmsnorm(x, g):
    s, d = x.shape
    tm = _tile(s, 256)
    return pl.pallas_call(
        _rmsnorm_kernel,
        grid=(s // tm,),
        in_specs=[pl.BlockSpec((tm, d), lambda i: (i, 0)), pl.BlockSpec((1, d), lambda i: (0, 0))],
        out_specs=pl.BlockSpec((tm, d), lambda i: (i, 0)),
        out_shape=jax.ShapeDtypeStruct((s, d), BF16),
        compiler_params=_params("parallel"),
        name="rmsnorm",
    )(x, _row(g))


def _residual_norm_kernel(y_ref, x_ref, gp_ref, gn_ref, xo_ref, ho_ref):
    y = y_ref[...].astype(F32)
    yn = y * lax.rsqrt(jnp.mean(y * y, axis=-1, keepdims=True) + EPS) * gp_ref[...]
    xn = x_ref[...] + yn
    xo_ref[...] = xn
    ho_ref[...] = (xn * lax.rsqrt(jnp.mean(xn * xn, axis=-1, keepdims=True) + EPS)
                   * gn_ref[...]).astype(ho_ref.dtype)


def _residual_only_kernel(y_ref, x_ref, gp_ref, xo_ref):
    y = y_ref[...].astype(F32)
    yn = y * lax.rsqrt(jnp.mean(y * y, axis=-1, keepdims=True) + EPS) * gp_ref[...]
    xo_ref[...] = x_ref[...] + yn


def residual_norm(y, x, g_post, g_next):
    s, d = x.shape
    tm = _tile(s, 256)
    blk = pl.BlockSpec((tm, d), lambda i: (i, 0))
    vec = pl.BlockSpec((1, d), lambda i: (0, 0))
    if g_next is None:
        return pl.pallas_call(
            _residual_only_kernel,
            grid=(s // tm,),
            in_specs=[blk, blk, vec],
            out_specs=blk,
            out_shape=jax.ShapeDtypeStruct((s, d), F32),
            compiler_params=_params("parallel"),
            name="residual_final",
        )(y, x, _row(g_post)), None
    return pl.pallas_call(
        _residual_norm_kernel,
        grid=(s // tm,),
        in_specs=[blk, blk, vec, vec],
        out_specs=[blk, blk],
        out_shape=[jax.ShapeDtypeStruct((s, d), F32), jax.ShapeDtypeStruct((s, d), BF16)],
        compiler_params=_params("parallel"),
        name="residual_norm",
    )(y, x, _row(g_post), _row(g_next))


def _mm_kernel(a_ref, w_ref, o_ref):
    o_ref[...] = jnp.dot(a_ref[...], w_ref[...], preferred_element_type=F32).astype(o_ref.dtype)


def _mm_nt_kernel(a_ref, wt_ref, o_ref):
    o_ref[...] = lax.dot_general(a_ref[...], wt_ref[...], (((1,), (1,)), ((), ())),
                                 preferred_element_type=F32).astype(o_ref.dtype)


def matmul_nt(a, wt, layer, *, tm=1024, tn=1024, name="matmul_nt"):
    m, kdim = a.shape
    n = wt.shape[1]
    tm = _tile(m, tm)
    tn = _tile(n, tn)
    return pl.pallas_call(
        _mm_nt_kernel,
        grid=(m // tm, n // tn),
        in_specs=[pl.BlockSpec((tm, kdim), lambda i, j: (i, 0)),
                  pl.BlockSpec((None, tn, kdim), lambda i, j: (layer, j, 0))],
        out_specs=pl.BlockSpec((tm, tn), lambda i, j: (i, j)),
        out_shape=jax.ShapeDtypeStruct((m, n), BF16),
        compiler_params=_params("parallel", "parallel"),
        name=name,
    )(a, wt)


def matmul(a, w, layer, *, tm=1024, tn=1024, name="matmul"):
    m, kdim = a.shape
    n = w.shape[2]
    tm = _tile(m, tm)
    tn = _tile(n, tn)
    return pl.pallas_call(
        _mm_kernel,
        grid=(m // tm, n // tn),
        in_specs=[pl.BlockSpec((tm, kdim), lambda i, j: (i, 0)),
                  pl.BlockSpec((None, kdim, tn), lambda i, j: (layer, 0, j))],
        out_specs=pl.BlockSpec((tm, tn), lambda i, j: (i, j)),
        out_shape=jax.ShapeDtypeStruct((m, n), BF16),
        compiler_params=_params("parallel", "parallel"),
        name=name,
    )(a, w)


def _dft_split(s):
    a = 1 << ((s.bit_length() - 1 + 1) // 2)
    assert s % a == 0
    return a, s // a


def _fourier_tables(s):
    a, b = _dft_split(s)
    two_pi = 2.0 * math.pi

    def cs(idx, period):
        ang = (idx % period).astype(F32) * (two_pi / period)
        return jnp.cos(ang), jnp.sin(ang)

    k1 = jnp.arange(a, dtype=jnp.int32)
    c1, s1 = cs(k1[:, None] * k1[None, :], a)
    w1 = jnp.concatenate([c1, -s1], axis=0) * (1.0 / math.sqrt(s))
    k2 = jnp.arange(b, dtype=jnp.int32)
    kk = k1[:, None, None] + a * k2[None, :, None]
    c2, s2 = cs(kk * k2[None, None, :], s)
    g = jnp.concatenate([jnp.concatenate([c2, s2], axis=2),
                         jnp.concatenate([-s2, c2], axis=2)], axis=1)
    c = jnp.arange(F_GROUP_DIM, dtype=jnp.int32)
    cc, sc = cs(c[:, None] * c[None, :], F_GROUP_DIM)
    wc = jnp.concatenate([cc, sc], axis=0) * (1.0 / math.sqrt(F_GROUP_DIM))
    return w1.astype(BF16), g.astype(BF16), wc.astype(BF16)


def _dft1_kernel(w_ref, z_ref, o_ref):
    o_ref[...] = jnp.dot(w_ref[...], z_ref[...], preferred_element_type=F32).astype(o_ref.dtype)


def _dft2_kernel(g_ref, t_ref, wc_ref, o_ref, *, nb, bdim):
    xr, xi = [], []
    for j in range(nb):
        t = t_ref[:, j].reshape(2 * bdim, F_WIDTH)
        x = jnp.dot(g_ref[j], t, preferred_element_type=F32)
        xr.append(x[:bdim].astype(BF16))
        xi.append(x[bdim:].astype(BF16))
    xr = jnp.concatenate(xr, axis=0)
    xi = jnp.concatenate(xi, axis=0)
    for grp in range(F_GROUPS):
        sl = slice(grp * F_GROUP_DIM, (grp + 1) * F_GROUP_DIM)
        y = (jnp.dot(xr[:, sl], wc_ref[:F_GROUP_DIM], preferred_element_type=F32)
             + jnp.dot(xi[:, sl], wc_ref[F_GROUP_DIM:], preferred_element_type=F32))
        for j in range(nb):
            o_ref[:, j * F_WIDTH + grp * F_GROUP_DIM: j * F_WIDTH + (grp + 1) * F_GROUP_DIM] = (
                y[j * bdim:(j + 1) * bdim].astype(o_ref.dtype))


def fourier_mix(zf, tables):
    s = zf.shape[0]
    a, b = _dft_split(s)
    w1, g, wc = tables
    cols = b * F_WIDTH
    tn = _tile(cols, 8 * F_WIDTH)
    t = pl.pallas_call(
        _dft1_kernel,
        grid=(cols // tn,),
        in_specs=[pl.BlockSpec((2 * a, a), lambda j: (0, 0)),
                  pl.BlockSpec((a, tn), lambda j: (0, j))],
        out_specs=pl.BlockSpec((2 * a, tn), lambda j: (0, j)),
        out_shape=jax.ShapeDtypeStruct((2 * a, cols), BF16),
        compiler_params=_params("parallel"),
        name="dft_stage1",
    )(w1, zf.reshape(a, cols))
    nb = _tile(a, 8)
    out = pl.pallas_call(
        functools.partial(_dft2_kernel, nb=nb, bdim=b),
        grid=(a // nb,),
        in_specs=[pl.BlockSpec((nb, 2 * b, 2 * b), lambda i: (i, 0, 0)),
                  pl.BlockSpec((2, nb, b, F_WIDTH), lambda i: (0, i, 0, 0)),
                  pl.BlockSpec((2 * F_GROUP_DIM, F_GROUP_DIM), lambda i: (0, 0))],
        out_specs=pl.BlockSpec((b, nb * F_WIDTH), lambda i: (0, i)),
        out_shape=jax.ShapeDtypeStruct((b, a * F_WIDTH), BF16),
        compiler_params=_params("parallel"),
        name="dft_stage2",
    )(g, t.reshape(2, a, b, F_WIDTH), wc)
    return out.reshape(s, F_WIDTH)


def _rope_lanes(y2, cs):
    w = y2 * cs
    return w + pltpu.roll(w, QK_ROPE, axis=1)


def _mla_q_kernel(cq_ref, gq_ref, cs_ref, w_ref, q_ref, *, scale):
    c = cq_ref[...].astype(F32)
    n = (c * lax.rsqrt(jnp.mean(c * c, axis=-1, keepdims=True) + EPS) * gq_ref[...]).astype(BF16)
    cs = cs_ref[...]
    for hd in range(MLA_HEADS):
        y = jnp.dot(n, w_ref[hd], preferred_element_type=F32) * scale
        q_ref[hd, :, :QK_NOPE] = y[:, :QK_NOPE].astype(q_ref.dtype)
        q_ref[hd, :, QK_NOPE:] = _rope_lanes(y[:, QK_NOPE:], cs).astype(q_ref.dtype)


def _mla_kv_kernel(ckv_ref, kr_ref, gkv_ref, cs_ref, w_ref, k_ref, v_ref):
    c = ckv_ref[...].astype(F32)
    n = (c * lax.rsqrt(jnp.mean(c * c, axis=-1, keepdims=True) + EPS) * gkv_ref[...]).astype(BF16)
    r = _rope_lanes(kr_ref[...].astype(F32), cs_ref[...])
    lane = lax.broadcasted_iota(jnp.int32, r.shape, 1)
    r = jnp.where(lane < QK_ROPE, r, 0.0).astype(k_ref.dtype)
    ones = jnp.ones((v_ref.shape[1], V_DIM), v_ref.dtype)
    for hd in range(MLA_HEADS):
        y = jnp.dot(n, w_ref[hd], preferred_element_type=F32)
        k_ref[hd, :, :QK_NOPE] = y[:, :QK_NOPE].astype(k_ref.dtype)
        k_ref[hd, :, QK_NOPE:] = r
        v_ref[hd, :, :V_DIM] = y[:, QK_NOPE:].astype(v_ref.dtype)
        v_ref[hd, :, V_DIM:] = ones


def _flash_kernel(q_ref, k_ref, v_ref, o_ref, sa_ref, sb_ref, m_ref, acc_ref, *, tk):
    nk = k_ref.shape[0] // tk
    q = q_ref[...]

    def scores(c):
        off = pl.multiple_of(c * tk, tk)
        return lax.dot_general(q, k_ref[pl.ds(off, tk), :], (((1,), (1,)), ((), ())),
                               preferred_element_type=F32)

    def accumulate(c, s_ref):
        off = pl.multiple_of(c * tk, tk)
        s = s_ref[...]
        m_prev = m_ref[...]
        m_new = jnp.maximum(m_prev, jnp.max(s, axis=-1, keepdims=True))
        p = jnp.exp2(s - m_new).astype(BF16)
        acc_ref[...] = (jnp.exp2(m_prev - m_new) * acc_ref[...]
                        + jnp.dot(p, v_ref[pl.ds(off, tk), :], preferred_element_type=F32))
        m_ref[...] = m_new

    m_ref[...] = jnp.full_like(m_ref, -jnp.inf)
    acc_ref[...] = jnp.zeros_like(acc_ref)
    sa_ref[...] = scores(0)

    bufs = (sa_ref, sb_ref)
    trips = (nk - 1) // FLASH_UNROLL

    def body(t, carry):
        a = FLASH_UNROLL * t
        for u in range(FLASH_UNROLL):
            bufs[(u + 1) % 2][...] = scores(a + u + 1)
            accumulate(a + u, bufs[u % 2])
        return carry

    lax.fori_loop(0, trips, body, 0)
    for c in range(trips * FLASH_UNROLL, nk):
        if c + 1 < nk:
            bufs[(c + 1) % 2][...] = scores(c + 1)
        accumulate(c, bufs[c % 2])
    acc = acc_ref[...]
    o_ref[...] = (acc[:, :V_DIM] / acc[:, V_DIM:]).astype(o_ref.dtype)


def mla_mix(cq, ckvr, cs, g_q, wq, g_kv, wkv):
    s = cq.shape[0]
    tm = _tile(s, 1024)
    scale = math.log2(math.e) / math.sqrt(QK_NOPE + QK_ROPE)
    dq = QK_NOPE + 2 * QK_ROPE
    dv = 2 * V_DIM
    q = pl.pallas_call(
        functools.partial(_mla_q_kernel, scale=scale),
        grid=(s // tm,),
        in_specs=[pl.BlockSpec((tm, Q_LORA), lambda i: (i, 0)),
                  pl.BlockSpec((1, Q_LORA), lambda i: (0, 0)),
                  pl.BlockSpec((tm, LANES), lambda i: (i, 0)),
                  pl.BlockSpec((MLA_HEADS, Q_LORA, dq), lambda i: (0, 0, 0))],
        out_specs=pl.BlockSpec((MLA_HEADS, tm, dq), lambda i: (0, i, 0)),
        out_shape=jax.ShapeDtypeStruct((MLA_HEADS, s, dq), BF16),
        compiler_params=_params("parallel"),
        name="mla_q",
    )(cq, _row(g_q), cs, wq)
    k, v = pl.pallas_call(
        _mla_kv_kernel,
        grid=(s // tm,),
        in_specs=[pl.BlockSpec((tm, KV_LORA), lambda i: (i, 0)),
                  pl.BlockSpec((tm, LANES), lambda i: (i, KV_LORA // LANES)),
                  pl.BlockSpec((1, KV_LORA), lambda i: (0, 0)),
                  pl.BlockSpec((tm, LANES), lambda i: (i, 0)),
                  pl.BlockSpec((MLA_HEADS, KV_LORA, QK_NOPE + V_DIM), lambda i: (0, 0, 0))],
        out_specs=[pl.BlockSpec((MLA_HEADS, tm, dq), lambda i: (0, i, 0)),
                   pl.BlockSpec((MLA_HEADS, tm, dv), lambda i: (0, i, 0))],
        out_shape=[jax.ShapeDtypeStruct((MLA_HEADS, s, dq), BF16),
                   jax.ShapeDtypeStruct((MLA_HEADS, s, dv), BF16)],
        compiler_params=_params("parallel"),
        name="mla_kv",
    )(ckvr, ckvr, _row(g_kv), cs, wkv)
    tq = _tile(s, 1024)
    tk = _tile(s, 1024)
    return pl.pallas_call(
        functools.partial(_flash_kernel, tk=tk),
        grid=(MLA_HEADS, s // tq),
        in_specs=[pl.BlockSpec((None, tq, dq), lambda h, i: (h, i, 0)),
                  pl.BlockSpec((None, s, dq), lambda h, i: (h, 0, 0)),
                  pl.BlockSpec((None, s, dv), lambda h, i: (h, 0, 0))],
        out_specs=pl.BlockSpec((tq, V_DIM), lambda h, i: (i, h)),
        out_shape=jax.ShapeDtypeStruct((s, MLA_HEADS * V_DIM), BF16),
        scratch_shapes=[pltpu.VMEM((tq, tk), F32), pltpu.VMEM((tq, tk), F32),
                        pltpu.VMEM((tq, 1), F32), pltpu.VMEM((tq, dv), F32)],
        compiler_params=_params("parallel", "arbitrary"),
        name="mla_flash",
    )(q, k, v)


GELU_C = math.sqrt(2.0 / math.pi)


def _gelu_tanh(x):
    return jnp.tanh(x * ((x * x) * (GELU_C * 0.044715) + GELU_C))


def _sg_kernel(u_ref, v_ref, g_ref, w_ref, b_ref, o_ref):
    def group(grp, carry):
        cols = pl.ds(pl.multiple_of(grp * SG_GROUP_DIM, SG_GROUP_DIM), SG_GROUP_DIM)
        v = v_ref[:, cols].astype(F32)
        hv = 0.5 * v
        v = hv + hv * _gelu_tanh(v)
        mu = jnp.mean(v, axis=-1, keepdims=True)
        vc = v - mu
        vn = (vc * lax.rsqrt(jnp.mean(vc * vc, axis=-1, keepdims=True) + EPS)
              * g_ref[:, cols]).astype(BF16)
        w = w_ref[grp]
        b = b_ref[grp]
        for c in range(u_ref.shape[0] // CHUNK):
            rows = slice(c * CHUNK, (c + 1) * CHUNK)
            sp_half = jnp.dot(w, vn[rows], preferred_element_type=F32) + b
            u = u_ref[rows, cols].astype(F32)
            o_ref[rows, cols] = ((u * sp_half) * (1.0 + _gelu_tanh(u))).astype(o_ref.dtype)
        return carry

    lax.fori_loop(0, SG_GROUPS, group, 0)


def spatial_gating(zu, zv, ln_gain, w_spatial, b_spatial):
    s = zu.shape[0]
    tm = _tile(s, 1024)
    blk = pl.BlockSpec((tm, SG_WIDTH), lambda i: (i, 0))
    return pl.pallas_call(
        _sg_kernel,
        grid=(s // tm,),
        in_specs=[blk, blk,
                  pl.BlockSpec((1, SG_WIDTH), lambda i: (0, 0)),
                  pl.BlockSpec((SG_GROUPS, CHUNK, CHUNK), lambda i: (0, 0, 0)),
                  pl.BlockSpec((SG_GROUPS, CHUNK, 1), lambda i: (0, 0, 0))],
        out_specs=blk,
        out_shape=jax.ShapeDtypeStruct((s, SG_WIDTH), BF16),
        compiler_params=_params("parallel"),
        name="spatial_gating",
    )(zu, zv, _row(ln_gain), (0.5 * w_spatial).astype(BF16), 0.5 * b_spatial.astype(F32)[:, :, None])


def _merge_kernel(h_ref, f_ref, a_ref, s_ref, wg0_ref, wg1_ref, wg2_ref, b0_ref, b1_ref, b2_ref,
                  wf_ref, wa_ref, ws_ref, o_ref):
    h = h_ref[...]

    def gated(wg_ref, b_ref, y_ref, w_ref):
        gate = jax.nn.sigmoid(jnp.dot(h, wg_ref[...], preferred_element_type=F32) + b_ref[...])
        return gate * jnp.dot(y_ref[...], w_ref[...], preferred_element_type=F32)

    o_ref[...] = (gated(wg0_ref, b0_ref, f_ref, wf_ref) + gated(wg1_ref, b1_ref, a_ref, wa_ref)
                  + gated(wg2_ref, b2_ref, s_ref, ws_ref)).astype(o_ref.dtype)


def gated_merge(h, yf, ya, ys, w_gate, b_gate, w_f, w_a, w_s, layer):
    s, d = h.shape
    tm = _tile(s, 1024)
    tn = _tile(d, 256)
    nj = d // tn
    bg = _row(b_gate)

    def rows(width):
        return pl.BlockSpec((tm, width), lambda i, j: (i, 0))

    def gcol(b):
        return pl.BlockSpec((None, d, tn), lambda i, j, b=b: (layer, 0, j + b * nj))

    def bcol(b):
        return pl.BlockSpec((1, tn), lambda i, j, b=b: (0, j + b * nj))

    def wcol(width):
        return pl.BlockSpec((None, width, tn), lambda i, j: (layer, 0, j))

    return pl.pallas_call(
        _merge_kernel,
        grid=(s // tm, nj),
        in_specs=[rows(d), rows(F_WIDTH), rows(MLA_HEADS * V_DIM), rows(SG_WIDTH),
                  gcol(0), gcol(1), gcol(2), bcol(0), bcol(1), bcol(2),
                  wcol(F_WIDTH), wcol(MLA_HEADS * V_DIM), wcol(SG_WIDTH)],
        out_specs=pl.BlockSpec((tm, tn), lambda i, j: (i, j)),
        out_shape=jax.ShapeDtypeStruct((s, d), BF16),
        compiler_params=_params("parallel", "parallel"),
        name="gated_merge",
    )(h, yf, ya, ys, w_gate, w_gate, w_gate, bg, bg, bg, w_f, w_a, w_s)


def _cross_attn_kernel(q_ref, k_ref, v_ref, o_ref, *, scale):
    for hd in range(MEM_HEADS):
        sl = slice(hd * MEM_HEAD_DIM, (hd + 1) * MEM_HEAD_DIM)
        s = lax.dot_general(q_ref[:, sl], k_ref[:, sl], (((1,), (1,)), ((), ())),
                            preferred_element_type=F32) * scale
        p = jnp.exp(s - jnp.max(s, axis=-1, keepdims=True))
        o = jnp.dot(p.astype(BF16), v_ref[:, sl], preferred_element_type=F32)
        o_ref[:, sl] = (o / jnp.sum(p, axis=-1, keepdims=True)).astype(o_ref.dtype)


def cross_attention(q, k, v):
    s, width = q.shape
    m = k.shape[0]
    tm = _tile(s, 1024)
    return pl.pallas_call(
        functools.partial(_cross_attn_kernel, scale=1.0 / math.sqrt(MEM_HEAD_DIM)),
        grid=(s // tm,),
        in_specs=[pl.BlockSpec((tm, width), lambda i: (i, 0)),
                  pl.BlockSpec((m, width), lambda i: (0, 0)),
                  pl.BlockSpec((m, width), lambda i: (0, 0))],
        out_specs=pl.BlockSpec((tm, width), lambda i: (i, 0)),
        out_shape=jax.ShapeDtypeStruct((s, width), BF16),
        compiler_params=_params("parallel"),
        name="cross_attention",
    )(q, k, v)


def _ffn_up_kernel(h_ref, hp_ref, hn_ref, wg_ref, wv_ref, cwg_ref, cwv_ref, cbg_ref, cbv_ref, o_ref,
                   hx_ref):
    i = pl.program_id(0)
    tm = h_ref.shape[0]

    @pl.when(pl.program_id(1) == 0)
    def _():
        hx_ref[0:HALO] = jnp.where(i > 0, hp_ref[...], jnp.zeros_like(hp_ref))
        hx_ref[HALO:HALO + tm] = h_ref[...]
        hx_ref[HALO + tm:] = jnp.where(i < pl.num_programs(0) - 1, hn_ref[...], jnp.zeros_like(hn_ref))

    hx = hx_ref[...]
    rows = hx.shape[0]

    def conv(w_ref, cw_ref, cb_ref):
        u = jnp.dot(hx, w_ref[...].astype(BF16), preferred_element_type=F32)
        above = pltpu.roll(u, 1, axis=0)[HALO:HALO + tm]
        below = pltpu.roll(u, rows - 1, axis=0)[HALO:HALO + tm]
        cw = cw_ref[...]
        return above * cw[0:1] + u[HALO:HALO + tm] * cw[1:2] + below * cw[2:3] + cb_ref[...]

    g = conv(wg_ref, cwg_ref, cbg_ref)
    v_half = conv(wv_ref, cwv_ref, cbv_ref)
    o_ref[...] = ((g * v_half) * (1.0 + _gelu_tanh(g))).astype(o_ref.dtype)


def ffn_up(h, w_up, conv_w, conv_b, layer):
    s, d = h.shape
    d_ff = w_up.shape[2] // 2
    tm = _tile(s, 1024)
    tn = _tile(d_ff, 256)
    nj = d_ff // tn
    hb = tm // HALO
    last = s // HALO - 1
    half = jnp.concatenate([jnp.ones((d_ff,), F32), jnp.full((d_ff,), 0.5, F32)])
    cb = _row(conv_b) * half
    cw = conv_w.astype(F32) * half
    return pl.pallas_call(
        _ffn_up_kernel,
        grid=(s // tm, nj),
        in_specs=[pl.BlockSpec((tm, d), lambda i, j: (i, 0)),
                  pl.BlockSpec((HALO, d), lambda i, j: (jnp.maximum(i * hb - 1, 0), 0)),
                  pl.BlockSpec((HALO, d), lambda i, j: (jnp.minimum((i + 1) * hb, last), 0)),
                  pl.BlockSpec((None, d, tn), lambda i, j: (layer, 0, j)),
                  pl.BlockSpec((None, d, tn), lambda i, j: (layer, 0, j + nj)),
                  pl.BlockSpec((3, tn), lambda i, j: (0, j)),
                  pl.BlockSpec((3, tn), lambda i, j: (0, j + nj)),
                  pl.BlockSpec((1, tn), lambda i, j: (0, j)),
                  pl.BlockSpec((1, tn), lambda i, j: (0, j + nj))],
        out_specs=pl.BlockSpec((tm, tn), lambda i, j: (i, j)),
        out_shape=jax.ShapeDtypeStruct((s, d_ff), BF16),
        scratch_shapes=[pltpu.VMEM((tm + 2 * HALO, d), BF16)],
        compiler_params=_params("parallel", "arbitrary"),
        name="ffn_up_conv",
    )(h, h, h, w_up, w_up, cw, cw, cb, cb)


def _split_w_in(w_in):
    half = QK_ROPE // 2
    wt = jnp.swapaxes(w_in, 1, 2)
    w_kr = wt[:, OFF_KR:OFF_SG]
    w_kr_rot = jnp.concatenate([-w_kr[:, half:], w_kr[:, :half]], axis=1)
    return (wt[:, :OFF_CQ], wt[:, OFF_CQ:OFF_CKV],
            jnp.concatenate([wt[:, OFF_CKV:OFF_KR], w_kr, w_kr_rot], axis=1),
            wt[:, OFF_SG:OFF_SG + SG_WIDTH], wt[:, OFF_SG + SG_WIDTH:])


def _prep_w_uq(w_uq):
    half = QK_ROPE // 2
    w = w_uq.reshape(Q_LORA, MLA_HEADS, QK_NOPE + QK_ROPE)
    rope = w[:, :, QK_NOPE:]
    rot = jnp.concatenate([-rope[:, :, half:], rope[:, :, :half]], axis=2)
    return jnp.concatenate([w, rot], axis=2).transpose(1, 0, 2).astype(BF16)


def _prep_w_ukv(w_ukv):
    return w_ukv.reshape(KV_LORA, MLA_HEADS, QK_NOPE + V_DIM).transpose(1, 0, 2).astype(BF16)


def kernel(x, mem, positions, mix_pre_norm, mix_post_norm, w_in, mla_q_norm, w_uq, mla_kv_norm, w_ukv, sg_norm, w_spatial, b_spatial, w_br_f, w_br_a, w_br_s, w_gate, b_gate, w_out, mem_pre_norm, mem_post_norm, mem_kv_norm, w_mq, w_mk, w_mv, w_mo, ffn_pre_norm, ffn_post_norm, w_up, conv_w, conv_b, w_down):
    batch, s, d = x.shape
    depth = w_in.shape[0]

    w_f, w_cq, w_ckvr, w_su, w_sv = [w.astype(BF16) for w in _split_w_in(w_in)]
    (w_gate, w_br_f, w_br_a, w_br_s, w_out, w_mq, w_mk, w_mv, w_mo, w_down) = [
        w.astype(BF16) for w in (w_gate, w_br_f, w_br_a, w_br_s, w_out, w_mq, w_mk, w_mv, w_mo,
                                 w_down)]

    inv_freq = ROPE_THETA ** (-jnp.arange(0, QK_ROPE, 2, dtype=F32) / QK_ROPE)
    tables = _fourier_tables(s)
    outs = []
    for b in range(batch):
        ang = positions[b].astype(F32)[:, None] * inv_freq
        cos, sin = jnp.cos(ang), jnp.sin(ang)
        cs = jnp.concatenate([cos, cos, sin, sin], axis=1)
        xb = x[b]
        mem_b = mem[b]
        h = rmsnorm(xb, mix_pre_norm[0])
        for l in range(depth):
            zf = matmul_nt(h, w_f, l, name="in_fourier")
            cq = matmul_nt(h, w_cq, l, name="in_cq")
            ckvr = matmul_nt(h, w_ckvr, l, name="in_ckv_krope")
            zu = matmul_nt(h, w_su, l, name="in_sg_u")
            zv = matmul_nt(h, w_sv, l, name="in_sg_v")
            yf = fourier_mix(zf, tables)
            ya = mla_mix(cq, ckvr, cs, mla_q_norm[l], _prep_w_uq(w_uq[l]),
                         mla_kv_norm[l], _prep_w_ukv(w_ukv[l]))
            ys = spatial_gating(zu, zv, sg_norm[l], w_spatial[l], b_spatial[l])
            merged = gated_merge(h, yf, ya, ys, w_gate, b_gate[l], w_br_f, w_br_a, w_br_s, l)
            y = matmul(merged, w_out, l, name="mix_out")
            xb, h = residual_norm(y, xb, mix_post_norm[l], mem_pre_norm[l])

            mem_n = rmsnorm(mem_b, mem_kv_norm[l])
            q = matmul(h, w_mq, l, name="mem_q")
            k = matmul(mem_n, w_mk, l, name="mem_k")
            v = matmul(mem_n, w_mv, l, name="mem_v")
            o = cross_attention(q, k, v)
            y = matmul(o, w_mo, l, name="mem_out")
            xb, h = residual_norm(y, xb, mem_post_norm[l], ffn_pre_norm[l])

            act = ffn_up(h, w_up, conv_w[l], conv_b[l], l)
            y = matmul(act, w_down, l, tm=512, tn=512, name="ffn_down")
            g_next = mix_pre_norm[l + 1] if l + 1 < depth else None
            xb, h = residual_norm(y, xb, ffn_post_norm[l], g_next)
        outs.append(xb)
    return outs[0][None] if batch == 1 else jnp.stack(outs, axis=0)
```
